```python
import math
import jax, jax.numpy as jnp
from jax import lax
import numpy as np

D_MODEL = 2048
BATCH = 1
SEQ = 16384
DEPTH = 2

CHUNK = 64
D_MIX = D_MODEL
N_MIXERS = 4
GROUP_W = D_MIX // N_MIXERS
LRU_HEADS = 4
LRU_BLOCK = GROUP_W // LRU_HEADS
LRU_CONV = 4
LRU_C = 8.0
SCONV_W = 3
GLA_HEADS = 4
GLA_DK = GROUP_W // (2 * GLA_HEADS)
GLA_DV = GROUP_W // GLA_HEADS
GLA_RANK = 16
GLA_TAU = 16.0
ATT_HEADS = 4
ATT_HD = GROUP_W // ATT_HEADS
ATT_PREV_CHUNKS = 8
ATT_BAND = ATT_PREV_CHUNKS + 1
REL_CLIP = 256
N_GROUPS = 4
EXPERTS_PER_GROUP = 4
N_EXPERTS = N_GROUPS * EXPERTS_PER_GROUP
TOP_K = 2
D_EXPERT = D_MODEL // 4
LN_EPS = 1e-5
RMS_EPS = 1e-6
DN_ALPHA = (2 * DEPTH) ** 0.25
DN_BETA = (8 * DEPTH) ** -0.25

IN_SPLITS = (GROUP_W, GROUP_W,
             GROUP_W, GROUP_W, GROUP_W,
             GLA_HEADS * GLA_DK, GLA_HEADS * GLA_DK, GROUP_W, GROUP_W, GLA_RANK,
             GROUP_W, GROUP_W, GROUP_W)
D_IN = sum(IN_SPLITS)
SPLIT_POINTS = tuple(int(p) for p in np.cumsum(IN_SPLITS)[:-1])

kernel_name = 'hybrid_parallel_mixers_hier_moe_deepnorm'


def layer_norm(x, g, b):
    xf = x.astype(jnp.float32)
    mu = jnp.mean(xf, -1, keepdims=True)
    var = jnp.mean(jnp.square(xf - mu), -1, keepdims=True)
    return ((xf - mu) * lax.rsqrt(var + LN_EPS)).astype(x.dtype) * g + b


def causal_dwconv(x, w):
    k, c = w.shape
    return lax.conv_general_dilated(x, w[:, None, :].astype(x.dtype), window_strides=(1,),
                                    padding=[(k - 1, 0)],
                                    dimension_numbers=('NWC', 'WIO', 'NWC'),
                                    feature_group_count=c)


def rg_lru(u, w_a, b_a, w_x, b_x, lam):
    bn, s, _ = u.shape
    ub = u.reshape(bn, s, LRU_HEADS, LRU_BLOCK)
    r = jax.nn.sigmoid(jnp.einsum('bshi,hij->bshj', ub, w_a).reshape(bn, s, GROUP_W) + b_a)
    i = jax.nn.sigmoid(jnp.einsum('bshi,hij->bshj', ub, w_x).reshape(bn, s, GROUP_W) + b_x)
    log_a = (-LRU_C * r.astype(jnp.float32)) * jax.nn.softplus(-lam.astype(jnp.float32))
    a = jnp.exp(log_a)
    mult = jnp.sqrt(-jnp.expm1(2.0 * log_a))
    bterm = mult * (i * u).astype(jnp.float32)

    def combine(c1, c2):
        a1, b1 = c1
        a2, b2 = c2
        return a1 * a2, a2 * b1 + b2

    _, h = lax.associative_scan(combine, (a, bterm), axis=1)
    return h.astype(u.dtype)


def gla(q, k, v, g, a_lr, w_a2, b_a2, norm_w):
    f32 = jnp.float32
    bn, s, _ = q.shape
    nc = s // CHUNK
    q = q.reshape(bn, nc, CHUNK, GLA_HEADS, GLA_DK).astype(f32) * (GLA_DK ** -0.5)
    k = k.reshape(bn, nc, CHUNK, GLA_HEADS, GLA_DK).astype(f32)
    v = v.reshape(bn, nc, CHUNK, GLA_HEADS, GLA_DV).astype(f32)
    log_alpha = jax.nn.log_sigmoid((a_lr @ w_a2 + b_a2).astype(f32)) / GLA_TAU
    log_alpha = log_alpha.reshape(bn, nc, CHUNK, GLA_HEADS, GLA_DK)
    bcum = jnp.cumsum(log_alpha, axis=2)
    b_last = bcum[:, :, -1]
    q_dec = q * jnp.exp(bcum)
    k_inv = k * jnp.exp(-bcum)
    k_end = k * jnp.exp(b_last[:, :, None] - bcum)
    scores = jnp.einsum('bnthk,bnshk->bnhts', q_dec, k_inv)
    causal = jnp.tril(jnp.ones((CHUNK, CHUNK), dtype=bool))
    scores = jnp.where(causal, scores, 0.0)
    o_intra = jnp.einsum('bnhts,bnshv->bnthv', scores, v)
    chunk_kv = jnp.einsum('bnshk,bnshv->nbhkv', k_end, v)
    decay = jnp.exp(b_last).transpose(1, 0, 2, 3)

    def step(state, inp):
        d, kv = inp
        return state * d[..., None] + kv, state

    s0 = jnp.zeros((bn, GLA_HEADS, GLA_DK, GLA_DV), f32)
    _, prev = lax.scan(step, s0, (decay, chunk_kv))
    o_inter = jnp.einsum('bnthk,nbhkv->bnthv', q_dec, prev)
    o = o_intra + o_inter
    o = o * lax.rsqrt(jnp.mean(o * o, -1, keepdims=True) + RMS_EPS)
    o = o.reshape(bn, s, GROUP_W).astype(g.dtype) * norm_w
    return o * jax.nn.silu(g)


def chunk_local_attention(q, k, v, rel_bias):
    f32 = jnp.float32
    bn, s, _ = q.shape
    nc = s // CHUNK
    q = q.reshape(bn, nc, CHUNK, ATT_HEADS, ATT_HD)
    k = k.reshape(bn, nc, CHUNK, ATT_HEADS, ATT_HD)
    v = v.reshape(bn, nc, CHUNK, ATT_HEADS, ATT_HD)
    pad = ((0, 0), (ATT_PREV_CHUNKS, 0), (0, 0), (0, 0), (0, 0))
    kp = jnp.pad(k, pad)
    vp = jnp.pad(v, pad)
    band_idx = jnp.arange(nc)[:, None] + jnp.arange(ATT_BAND)[None, :]
    kb = kp[:, band_idx].reshape(bn, nc, ATT_BAND * CHUNK, ATT_HEADS, ATT_HD)
    vb = vp[:, band_idx].reshape(bn, nc, ATT_BAND * CHUNK, ATT_HEADS, ATT_HD)
    scores = jnp.einsum('bnqhd,bnkhd->bnhqk', q, kb).astype(f32) * (ATT_HD ** -0.5)
    qpos = jnp.arange(CHUNK)
    kpos = jnp.arange(ATT_BAND * CHUNK) - ATT_PREV_CHUNKS * CHUNK
    rel_idx = jnp.clip(qpos[:, None] - kpos[None, :], -REL_CLIP, REL_CLIP) + REL_CLIP
    bias = rel_bias[:, rel_idx].astype(f32)
    valid = (jnp.arange(nc)[:, None] - ATT_PREV_CHUNKS + jnp.arange(ATT_BAND)[None, :]) >= 0
    valid = jnp.repeat(valid, CHUNK, axis=1)
    scores = jnp.where(valid[None, :, None, None, :], scores + bias, -jnp.inf)
    p = jax.nn.softmax(scores, axis=-1).astype(v.dtype)
    o = jnp.einsum('bnhqk,bnkhd->bnqhd', p, vb)
    return o.reshape(bn, s, GROUP_W)


def hier_moe(x, w_rg, b_rg, w_re, b_re, w_gate, w_up, w_down):
    f32 = jnp.float32
    bn, s, d = x.shape
    xt = x.reshape(-1, d)
    g_prob = jax.nn.softmax((xt @ w_rg).astype(f32) + b_rg.astype(f32), axis=-1)
    g_val, g_idx = lax.top_k(g_prob, 1)
    e_logits = ((xt @ w_re).astype(f32) + b_re.astype(f32)).reshape(-1, N_GROUPS, EXPERTS_PER_GROUP)
    g_onehot = jax.nn.one_hot(g_idx[:, 0], N_GROUPS, dtype=f32)
    e_in_group = jnp.einsum('tg,tge->te', g_onehot, e_logits)
    e_val, e_loc = lax.top_k(e_in_group, TOP_K)
    e_w = jax.nn.softmax(e_val, axis=-1) * g_val
    e_idx = g_idx * EXPERTS_PER_GROUP + e_loc
    combine = jnp.sum(jax.nn.one_hot(e_idx, N_EXPERTS, dtype=f32) * e_w[..., None], axis=1)
    combine = combine.astype(x.dtype)
    out = jnp.zeros_like(xt)
    for e in range(N_EXPERTS):
        h = jax.nn.silu(xt @ w_gate[e]) * (xt @ w_up[e])
        out = out + combine[:, e:e + 1] * (h @ w_down[e])
    return out.reshape(bn, s, d)


def setup_inputs(seed: int = 0) -> dict:
    key = jax.random.key(seed)
    ks = jax.random.split(key, 32)
    f32 = jnp.float32
    L = DEPTH

    def nrm(k, shape, scale):
        return jax.random.normal(k, shape, f32) * scale

    a_c = jax.random.uniform(ks[9], (L, GROUP_W), f32, 0.9, 0.999)
    a_base = a_c ** (1.0 / LRU_C)
    lru_lam = jnp.log(a_base) - jnp.log1p(-a_base)
    return {
        'x': nrm(ks[0], (BATCH, SEQ, D_MODEL), 1.0),
        'ln0_g': 1.0 + nrm(ks[1], (D_MODEL,), 0.02),
        'ln0_b': nrm(ks[2], (D_MODEL,), 0.02),
        'w_in': nrm(ks[3], (L, D_MODEL, D_IN), D_MODEL ** -0.5),
        'conv_a_w': nrm(ks[4], (L, LRU_CONV, GROUP_W), LRU_CONV ** -0.5),
        'conv_a_b': nrm(ks[5], (L, GROUP_W), 0.02),
        'lru_wa': nrm(ks[6], (L, LRU_HEADS, LRU_BLOCK, LRU_BLOCK), LRU_BLOCK ** -0.5),
        'lru_ba': nrm(ks[7], (L, GROUP_W), 0.02),
        'lru_wx': nrm(ks[8], (L, LRU_HEADS, LRU_BLOCK, LRU_BLOCK), LRU_BLOCK ** -0.5),
        'lru_bx': nrm(ks[10], (L, GROUP_W), 0.02),
        'lru_lam': lru_lam,
        'conv_b_w': nrm(ks[11], (L, SCONV_W, GROUP_W), SCONV_W ** -0.5),
        'gla_wa2': nrm(ks[12], (L, GLA_RANK, GLA_HEADS * GLA_DK), GLA_RANK ** -0.5),
        'gla_ba2': nrm(ks[13], (L, GLA_HEADS * GLA_DK), 0.02),
        'gla_norm_w': 1.0 + nrm(ks[14], (L, GROUP_W), 0.02),
        'att_rel_bias': nrm(ks[15], (L, ATT_HEADS, 2 * REL_CLIP + 1), 0.1),
        'w_out': nrm(ks[16], (L, D_MIX, D_MODEL), DN_BETA * D_MIX ** -0.5),
        'ln1_g': 1.0 + nrm(ks[17], (L, D_MODEL), 0.02),
        'ln1_b': nrm(ks[18], (L, D_MODEL), 0.02),
        'w_rg': nrm(ks[19], (L, D_MODEL, N_GROUPS), D_MODEL ** -0.5),
        'b_rg': nrm(ks[20], (L, N_GROUPS), 0.01),
        'w_re': nrm(ks[21], (L, D_MODEL, N_EXPERTS), D_MODEL ** -0.5),
        'b_re': nrm(ks[22], (L, N_EXPERTS), 0.01),
        'w_gate': nrm(ks[23], (L, N_EXPERTS, D_MODEL, D_EXPERT), D_MODEL ** -0.5),
        'w_up': nrm(ks[24], (L, N_EXPERTS, D_MODEL, D_EXPERT), D_MODEL ** -0.5),
        'w_down': nrm(ks[25], (L, N_EXPERTS, D_EXPERT, D_MODEL), DN_BETA * D_EXPERT ** -0.5),
        'ln2_g': 1.0 + nrm(ks[26], (L, D_MODEL), 0.02),
        'ln2_b': nrm(ks[27], (L, D_MODEL), 0.02),
    }


def reference(x, ln0_g, ln0_b, w_in, conv_a_w, conv_a_b, lru_wa, lru_ba, lru_wx, lru_bx, lru_lam,
              conv_b_w, gla_wa2, gla_ba2, gla_norm_w, att_rel_bias, w_out, ln1_g, ln1_b,
              w_rg, b_rg, w_re, b_re, w_gate, w_up, w_down, ln2_g, ln2_b):
    h = layer_norm(x, ln0_g, ln0_b)
    for l in range(DEPTH):
        z = h @ w_in[l]
        (a_x, a_g, b_b, b_c, b_h, c_q, c_k, c_v, c_g, c_a, d_q, d_k, d_v) = jnp.split(z, SPLIT_POINTS, axis=-1)
        u = causal_dwconv(a_x, conv_a_w[l]) + conv_a_b[l]
        y_a = rg_lru(u, lru_wa[l], lru_ba[l], lru_wx[l], lru_bx[l], lru_lam[l]) * jax.nn.gelu(a_g)
        y_b = b_b * causal_dwconv(b_c * b_h, conv_b_w[l])
        y_c = gla(c_q, c_k, c_v, c_g, c_a, gla_wa2[l], gla_ba2[l], gla_norm_w[l])
        y_d = chunk_local_attention(d_q, d_k, d_v, att_rel_bias[l])
        mix = jnp.concatenate([y_a, y_b, y_c, y_d], axis=-1) @ w_out[l]
        h = layer_norm(DN_ALPHA * h + mix, ln1_g[l], ln1_b[l])
        ffn = hier_moe(h, w_rg[l], b_rg[l], w_re[l], b_re[l], w_gate[l], w_up[l], w_down[l])
        h = layer_norm(DN_ALPHA * h + ffn, ln2_g[l], ln2_b[l])
    return h
```

```python
import functools
import math

import jax
import jax.numpy as jnp
import numpy as np
from jax import lax
from jax.experimental import pallas as pl
from jax.experimental.pallas import tpu as pltpu

F32 = jnp.float32
BF16 = jnp.bfloat16

D_MODEL = 2048
DEPTH = 2
CHUNK = 64
GROUP_W = 512
LRU_HEADS = 4
LRU_BLOCK = 128
LRU_CONV = 4
LRU_C = 8.0
SCONV_W = 3
GLA_HEADS = 4
GLA_DK = 64
GLA_DV = 128
GLA_RANK = 16
GLA_TAU = 16.0
ATT_HEADS = 4
ATT_HD = 128
ATT_PREV_CHUNKS = 8
REL_CLIP = 256
N_GROUPS = 4
EXPERTS_PER_GROUP = 4
N_EXPERTS = 16
D_EXPERT = 512
LN_EPS = 1e-5
RMS_EPS = 1e-6
DN_ALPHA = (2 * DEPTH) ** 0.25

LANES = 128
SUBLANES = 8
VMEM_LIMIT = 56 * 1024 * 1024

Z_W = 6144
ZB_AX, ZB_AG, ZB_BB, ZB_BC, ZB_BH, ZB_CQK, ZB_CV, ZB_CG, ZB_DQ, ZB_DK, ZB_DV = range(11)
ZB_CA128 = 44


def _cparams(sem):
    return pltpu.CompilerParams(dimension_semantics=sem, vmem_limit_bytes=VMEM_LIMIT)


def _layer_norm(x, g, b):
    mu = jnp.mean(x, axis=-1, keepdims=True)
    xc = x - mu
    var = jnp.mean(xc * xc, axis=-1, keepdims=True)
    return xc * lax.rsqrt(var + LN_EPS) * g + b


def _sigmoid(x):
    return 1.0 / (1.0 + jnp.exp(-x))


def _softplus(x):
    return jnp.maximum(x, 0.0) + jnp.log(1.0 + jnp.exp(-jnp.abs(x)))


def _silu(x):
    return x * _sigmoid(x)


def _gelu_tanh(x):
    c = math.sqrt(2.0 / math.pi)
    return 0.5 * x * (1.0 + jnp.tanh(c * (x + 0.044715 * (x * x * x))))


def _ln_kernel(x_ref, g_ref, b_ref, o_ref):
    o_ref[...] = _layer_norm(x_ref[...], g_ref[...], b_ref[...])


def _ln_rows(x, g, b, tm=512):
    t, d = x.shape
    return pl.pallas_call(
        _ln_kernel,
        grid=(t // tm,),
        in_specs=[pl.BlockSpec((tm, d), lambda i: (i, 0)),
                  pl.BlockSpec((1, d), lambda i: (0, 0)),
                  pl.BlockSpec((1, d), lambda i: (0, 0))],
        out_specs=pl.BlockSpec((tm, d), lambda i: (i, 0)),
        out_shape=jax.ShapeDtypeStruct((t, d), F32),
        compiler_params=_cparams(("arbitrary",)),
        name="ln0",
    )(x, g.reshape(1, d), b.reshape(1, d))


def _inproj_kernel(h_ref, w_ref, z_ref, hb_ref):
    @pl.when(pl.program_id(1) == 0)
    def _():
        hb_ref[...] = h_ref[...].astype(BF16)

    z_ref[...] = jnp.dot(hb_ref[...], w_ref[...], preferred_element_type=F32).astype(z_ref.dtype)


def _inproj(h, w_bf, tm=1024, tn=1024):
    t, d = h.shape
    n = w_bf.shape[1]
    tm = min(tm, t)
    return pl.pallas_call(
        _inproj_kernel,
        grid=(t // tm, n // tn),
        in_specs=[pl.BlockSpec((tm, d), lambda i, j: (i, 0)),
                  pl.BlockSpec((d, tn), lambda i, j: (0, j))],
        out_specs=pl.BlockSpec((tm, tn), lambda i, j: (i, j)),
        out_shape=jax.ShapeDtypeStruct((t, n), BF16),
        scratch_shapes=[pltpu.VMEM((tm, d), BF16)],
        compiler_params=_cparams(("arbitrary", "arbitrary")),
        name="inproj",
    )(h, w_bf)


def _ab_kernel(ax_ref, ag_ref, bb_ref, bc_ref, bh_ref,
               caw_ref, cab_ref, wa_ref, ba_ref, wx_ref, bx_ref, lam_ref, cbw_ref,
               ya_ref, yb_ref, xa_buf, xb_buf, h_st):
    tb = ax_ref.shape[0]
    pad = SUBLANES

    @pl.when(pl.program_id(0) == 0)
    def _():
        xa_buf[0:pad, :] = jnp.zeros((pad, GROUP_W), F32)
        xb_buf[0:pad, :] = jnp.zeros((pad, GROUP_W), F32)
        h_st[...] = jnp.zeros_like(h_st)

    xa_buf[pad:pad + tb, :] = ax_ref[...].astype(F32)
    u = cab_ref[...] + caw_ref[0:1, :] * xa_buf[pad - 3:pad - 3 + tb, :]
    for j in range(1, LRU_CONV):
        off = pad - (LRU_CONV - 1) + j
        u = u + caw_ref[j:j + 1, :] * xa_buf[off:off + tb, :]
    xa_buf[0:pad, :] = xa_buf[tb:tb + pad, :]

    ub = u.astype(BF16)
    r_parts, i_parts = [], []
    for hd in range(LRU_HEADS):
        sl = slice(hd * LRU_BLOCK, (hd + 1) * LRU_BLOCK)
        r_parts.append(jnp.dot(ub[:, sl], wa_ref[hd], preferred_element_type=F32))
        i_parts.append(jnp.dot(ub[:, sl], wx_ref[hd], preferred_element_type=F32))
    r = _sigmoid(jnp.concatenate(r_parts, axis=1) + ba_ref[...])
    ig = _sigmoid(jnp.concatenate(i_parts, axis=1) + bx_ref[...])

    log_a = (-LRU_C * r) * _softplus(-lam_ref[...])
    a = jnp.exp(log_a)
    th = jnp.tanh(log_a)
    mult = jnp.sqrt(-2.0 * th / (1.0 - th))
    bterm = mult * (ig * u)

    rows = lax.broadcasted_iota(jnp.int32, a.shape, 0)
    acum = a
    bcum = bterm
    d = 1
    while d < tb:
        head = rows < d
        b_sh = jnp.where(head, 0.0, pltpu.roll(bcum, d, axis=0))
        a_sh = jnp.where(head, 1.0, pltpu.roll(acum, d, axis=0))
        bcum = acum * b_sh + bcum
        acum = acum * a_sh
        d *= 2
    h = acum * h_st[0:1, :] + bcum
    h_st[0:1, :] = h[tb - 1:tb, :]
    ya_ref[...] = (h * _gelu_tanh(ag_ref[...].astype(F32))).astype(ya_ref.dtype)

    xb_buf[pad:pad + tb, :] = bc_ref[...].astype(F32) * bh_ref[...].astype(F32)
    cv = cbw_ref[0:1, :] * xb_buf[pad - 2:pad - 2 + tb, :]
    for j in range(1, SCONV_W):
        off = pad - (SCONV_W - 1) + j
        cv = cv + cbw_ref[j:j + 1, :] * xb_buf[off:off + tb, :]
    xb_buf[0:pad, :] = xb_buf[tb:tb + pad, :]
    yb_ref[...] = (bb_ref[...].astype(F32) * cv).astype(yb_ref.dtype)


def _mixer_ab(z, caw, cab, wa_bf, ba, wx_bf, bx, lam, cbw, tb=256):
    t = z.shape[0]
    tb = min(tb, t)
    w = GROUP_W

    def zspec(blk):
        return pl.BlockSpec((tb, w), lambda i, blk=blk: (i, blk))

    def full(shape):
        nd = len(shape)
        return pl.BlockSpec(shape, lambda i, nd=nd: (0,) * nd)

    return pl.pallas_call(
        _ab_kernel,
        grid=(t // tb,),
        in_specs=[zspec(ZB_AX), zspec(ZB_AG), zspec(ZB_BB), zspec(ZB_BC), zspec(ZB_BH),
                  full((LRU_CONV, w)), full((1, w)),
                  full((LRU_HEADS, LRU_BLOCK, LRU_BLOCK)), full((1, w)),
                  full((LRU_HEADS, LRU_BLOCK, LRU_BLOCK)), full((1, w)),
                  full((1, w)), full((SCONV_W, w))],
        out_specs=[pl.BlockSpec((tb, w), lambda i: (i, 0)),
                   pl.BlockSpec((tb, w), lambda i: (i, 0))],
        out_shape=[jax.ShapeDtypeStruct((t, w), BF16), jax.ShapeDtypeStruct((t, w), BF16)],
        scratch_shapes=[pltpu.VMEM((tb + SUBLANES, w), F32),
                        pltpu.VMEM((tb + SUBLANES, w), F32),
                        pltpu.VMEM((SUBLANES, w), F32)],
        compiler_params=_cparams(("arbitrary",)),
        name="mixer_ab",
    )(z, z, z, z, z, caw, cab.reshape(1, w), wa_bf, ba.reshape(1, w), wx_bf, bx.reshape(1, w),
      lam.reshape(1, w), cbw)


_NT = (((1,), (1,)), ((), ()))
_TN = (((0,), (0,)), ((), ()))


def _split_bf16(x):
    hi = x.astype(BF16)
    lo = (x - hi.astype(F32)).astype(BF16)
    return hi, lo


def _gla_kernel(qk_ref, v_ref, g_ref, ca_ref, wa2_ref, ba2_ref, nw_ref, y_ref, s_ref):
    tb = qk_ref.shape[0]
    hk = GLA_HEADS * GLA_DK
    hv = GLA_HEADS * GLA_DV
    L = CHUNK

    @pl.when(pl.program_id(0) == 0)
    def _():
        s_ref[...] = jnp.zeros_like(s_ref)

    tri = (lax.broadcasted_iota(jnp.int32, (L, L), 0) >= lax.broadcasted_iota(jnp.int32, (L, L), 1))
    tri_bf = tri.astype(BF16)
    tri4 = jnp.concatenate([tri] * GLA_HEADS, axis=0)
    lane_head = lax.broadcasted_iota(jnp.int32, (L, hk), 1) // GLA_DK
    bd_mask = (lax.broadcasted_iota(jnp.int32, (hk, hv), 0) // GLA_DK
               == lax.broadcasted_iota(jnp.int32, (hk, hv), 1) // GLA_DV)
    ones_bf = jnp.ones((L, LANES), BF16)
    scale = GLA_DK ** -0.5

    for c in range(tb // L):
        rs = slice(c * L, (c + 1) * L)
        q = qk_ref[rs, 0:hk].astype(F32) * scale
        k = qk_ref[rs, hk:2 * hk].astype(F32)
        v = v_ref[rs, :]
        x = jnp.dot(ca_ref[rs, :], wa2_ref[...], preferred_element_type=F32) + ba2_ref[...]
        la = (jnp.minimum(x, 0.0) - jnp.log(1.0 + jnp.exp(-jnp.abs(x)))) * (1.0 / GLA_TAU)
        la_hi, la_lo = _split_bf16(la)
        bcum = (jnp.dot(tri_bf, la_hi, preferred_element_type=F32)
                + jnp.dot(tri_bf, la_lo, preferred_element_type=F32))
        blast = bcum[L - 1:L, :]
        q_dec = q * jnp.exp(bcum)
        k_inv = (k * jnp.exp(-bcum)).astype(BF16)
        k_end = (k * jnp.exp(blast - bcum)).astype(BF16)

        q4 = jnp.concatenate(
            [jnp.where(lane_head == hd, q_dec, 0.0) for hd in range(GLA_HEADS)], axis=0).astype(BF16)
        sc = lax.dot_general(q4, k_inv, _NT, preferred_element_type=F32)
        sc = jnp.where(tri4, sc, 0.0).astype(BF16)
        oi = jnp.dot(sc, v, preferred_element_type=F32)
        o_intra = jnp.concatenate(
            [oi[hd * L:(hd + 1) * L, hd * GLA_DV:(hd + 1) * GLA_DV] for hd in range(GLA_HEADS)], axis=1)

        s = s_ref[...]
        o = o_intra + jnp.dot(q_dec.astype(BF16), s.astype(BF16), preferred_element_type=F32)

        kv = lax.dot_general(k_end, v, _TN, preferred_element_type=F32)
        bl_col = (lax.dot_general(la_hi, ones_bf, _TN, preferred_element_type=F32)
                  + lax.dot_general(la_lo, ones_bf, _TN, preferred_element_type=F32))
        dcol = jnp.exp(bl_col)
        s_ref[...] = s * jnp.concatenate([dcol] * (hv // LANES), axis=1) + jnp.where(bd_mask, kv, 0.0)

        parts = []
        for hd in range(GLA_HEADS):
            oh = o[:, hd * GLA_DV:(hd + 1) * GLA_DV]
            parts.append(oh * lax.rsqrt(jnp.mean(oh * oh, axis=-1, keepdims=True) + RMS_EPS))
        on = jnp.concatenate(parts, axis=1) * nw_ref[...]
        y_ref[rs, :] = (on * _silu(g_ref[rs, :].astype(F32))).astype(y_ref.dtype)


def _mixer_gla(z, wa2_pad_bf, ba2, nw, tb=256):
    t = z.shape[0]
    tb = min(tb, t)
    w = GROUP_W
    hk = GLA_HEADS * GLA_DK
    return pl.pallas_call(
        _gla_kernel,
        grid=(t // tb,),
        in_specs=[pl.BlockSpec((tb, w), lambda i: (i, ZB_CQK)),
                  pl.BlockSpec((tb, w), lambda i: (i, ZB_CV)),
                  pl.BlockSpec((tb, w), lambda i: (i, ZB_CG)),
                  pl.BlockSpec((tb, LANES), lambda i: (i, ZB_CA128)),
                  pl.BlockSpec((LANES, hk), lambda i: (0, 0)),
                  pl.BlockSpec((1, hk), lambda i: (0, 0)),
                  pl.BlockSpec((1, w), lambda i: (0, 0))],
        out_specs=pl.BlockSpec((tb, w), lambda i: (i, 0)),
        out_shape=jax.ShapeDtypeStruct((t, w), BF16),
        scratch_shapes=[pltpu.VMEM((hk, w), F32)],
        compiler_params=_cparams(("arbitrary",)),
        name="mixer_gla",
    )(z, z, z, z, wa2_pad_bf, ba2.reshape(1, hk), nw.reshape(1, w))


ATT_TQ = 512
ATT_SUB = 128
ATT_KW = ATT_SUB + ATT_PREV_CHUNKS * CHUNK


def _att_kernel(q_ref, kp_ref, kc_ref, vp_ref, vc_ref, bias_ref, y_ref, kcat, vcat):
    i = pl.program_id(0)
    kcat[0:ATT_TQ, :] = kp_ref[...]
    kcat[ATT_TQ:2 * ATT_TQ, :] = kc_ref[...]
    vcat[0:ATT_TQ, :] = vp_ref[...]
    vcat[ATT_TQ:2 * ATT_TQ, :] = vc_ref[...]
    col = lax.broadcasted_iota(jnp.int32, (ATT_SUB, ATT_KW), 1)
    scale = ATT_HD ** -0.5
    for j in range(ATT_TQ // ATT_SUB):
        r0 = j * ATT_SUB
        n_invalid = jnp.where(i == 0, ATT_TQ - r0, 0)
        dead = col < n_invalid
        outs = []
        for hd in range(ATT_HEADS):
            ls = slice(hd * ATT_HD, (hd + 1) * ATT_HD)
            qh = q_ref[r0:r0 + ATT_SUB, ls]
            kh = kcat[r0:r0 + ATT_KW, ls]
            vh = vcat[r0:r0 + ATT_KW, ls]
            s = lax.dot_general(qh, kh, _NT, preferred_element_type=F32) * scale + bias_ref[hd]
            s = jnp.where(dead, -jnp.inf, s)
            m = jnp.max(s, axis=-1, keepdims=True)
            p = jnp.exp(s - m)
            l = jnp.sum(p, axis=-1, keepdims=True)
            o = jnp.dot(p.astype(BF16), vh, preferred_element_type=F32)
            outs.append(o / l)
        y_ref[r0:r0 + ATT_SUB, :] = jnp.concatenate(outs, axis=1).astype(y_ref.dtype)


def _att_bias_table(rel_bias):
    r = np.arange(ATT_SUB)[:, None]
    c = np.arange(ATT_KW)[None, :]
    idx = np.clip(r - c + ATT_PREV_CHUNKS * CHUNK, -REL_CLIP, REL_CLIP) + REL_CLIP
    rel = c - CHUNK * (r // CHUNK)
    valid = (rel >= 0) & (rel < (ATT_PREV_CHUNKS + 1) * CHUNK)
    tab = rel_bias[:, idx].astype(F32)
    return jnp.where(jnp.asarray(valid)[None], tab, -jnp.inf)


def _mixer_att(z, bias_tab):
    t = z.shape[0]
    w = GROUP_W
    tq = ATT_TQ

    def prev(i):
        return jnp.maximum(i - 1, 0)

    return pl.pallas_call(
        _att_kernel,
        grid=(t // tq,),
        in_specs=[pl.BlockSpec((tq, w), lambda i: (i, ZB_DQ)),
                  pl.BlockSpec((tq, w), lambda i: (prev(i), ZB_DK)),
                  pl.BlockSpec((tq, w), lambda i: (i, ZB_DK)),
                  pl.BlockSpec((tq, w), lambda i: (prev(i), ZB_DV)),
                  pl.BlockSpec((tq, w), lambda i: (i, ZB_DV)),
                  pl.BlockSpec((ATT_HEADS, ATT_SUB, ATT_KW), lambda i: (0, 0, 0))],
        out_specs=pl.BlockSpec((tq, w), lambda i: (i, 0)),
        out_shape=jax.ShapeDtypeStruct((t, w), BF16),
        scratch_shapes=[pltpu.VMEM((2 * tq, w), BF16), pltpu.VMEM((2 * tq, w), BF16)],
        compiler_params=_cparams(("arbitrary",)),
        name="mixer_att",
    )(z, z, z, z, z, bias_tab)


R_ROWS = 32


def _outproj_kernel(ya_ref, yb_ref, yc_ref, yd_ref, w_ref, h_ref, g_ref, b_ref, wr_ref, br_ref,
                    h1_ref, comb_ref):
    w = GROUP_W
    acc = jnp.dot(ya_ref[...], w_ref[0:w, :], preferred_element_type=F32)
    acc = acc + jnp.dot(yb_ref[...], w_ref[w:2 * w, :], preferred_element_type=F32)
    acc = acc + jnp.dot(yc_ref[...], w_ref[2 * w:3 * w, :], preferred_element_type=F32)
    acc = acc + jnp.dot(yd_ref[...], w_ref[3 * w:4 * w, :], preferred_element_type=F32)
    h1 = _layer_norm(DN_ALPHA * h_ref[...] + acc, g_ref[...], b_ref[...])
    h1_ref[...] = h1

    h1_hi, h1_lo = _split_bf16(h1)
    lt = (lax.dot_general(wr_ref[0], h1_hi, _NT, preferred_element_type=F32)
          + lax.dot_general(wr_ref[0], h1_lo, _NT, preferred_element_type=F32)
          + lax.dot_general(wr_ref[1], h1_hi, _NT, preferred_element_type=F32)) + br_ref[:, 0:1]
    ng = N_GROUPS
    epg = EXPERTS_PER_GROUP
    lg = lt[0:ng, :]
    gm = jnp.max(lg, axis=0, keepdims=True)
    g_val = 1.0 / jnp.sum(jnp.exp(lg - gm), axis=0, keepdims=True)
    gi = jnp.full(gm.shape, ng - 1, jnp.int32)
    for g in range(ng - 2, -1, -1):
        gi = jnp.where(lg[g:g + 1, :] == gm, g, gi)
    e = []
    for j in range(epg):
        ej = lt[ng + (ng - 1) * epg + j:ng + (ng - 1) * epg + j + 1, :]
        for g in range(ng - 2, -1, -1):
            ej = jnp.where(gi == g, lt[ng + g * epg + j:ng + g * epg + j + 1, :], ej)
        e.append(ej)
    v1 = jnp.maximum(jnp.maximum(e[0], e[1]), jnp.maximum(e[2], e[3]))
    i1 = jnp.full(v1.shape, epg - 1, jnp.int32)
    for j in range(epg - 2, -1, -1):
        i1 = jnp.where(e[j] == v1, j, i1)
    neg = -jnp.inf
    e2 = [jnp.where(i1 == j, neg, e[j]) for j in range(epg)]
    v2 = jnp.maximum(jnp.maximum(e2[0], e2[1]), jnp.maximum(e2[2], e2[3]))
    i2 = jnp.full(v1.shape, epg - 1, jnp.int32)
    for j in range(epg - 2, -1, -1):
        i2 = jnp.where(e2[j] == v2, j, i2)
    t2 = jnp.exp(v2 - v1)
    w1 = g_val / (1.0 + t2)
    w2 = g_val * t2 / (1.0 + t2)
    rows = lax.broadcasted_iota(jnp.int32, (N_EXPERTS, lt.shape[1]), 0)
    e1g = gi * epg + i1
    e2g = gi * epg + i2
    comb_ref[...] = jnp.where(rows == e1g, w1, 0.0) + jnp.where(rows == e2g, w2, 0.0)


def _outproj_router(ya, yb, yc, yd, w_out_bf, h, g, b, wr_t_bf, br_col, tm=512):
    t, d = h.shape
    w = GROUP_W
    tm = min(tm, t)

    def ys():
        return pl.BlockSpec((tm, w), lambda i: (i, 0))

    return pl.pallas_call(
        _outproj_kernel,
        grid=(t // tm,),
        in_specs=[ys(), ys(), ys(), ys(),
                  pl.BlockSpec((d, d), lambda i: (0, 0)),
                  pl.BlockSpec((tm, d), lambda i: (i, 0)),
                  pl.BlockSpec((1, d), lambda i: (0, 0)),
                  pl.BlockSpec((1, d), lambda i: (0, 0)),
                  pl.BlockSpec((2, R_ROWS, d), lambda i: (0, 0, 0)),
                  pl.BlockSpec((R_ROWS, LANES), lambda i: (0, 0))],
        out_specs=[pl.BlockSpec((tm, d), lambda i: (i, 0)),
                   pl.BlockSpec((N_EXPERTS, tm), lambda i: (0, i))],
        out_shape=[jax.ShapeDtypeStruct((t, d), F32),
                   jax.ShapeDtypeStruct((N_EXPERTS, t), F32)],
        compiler_params=_cparams(("arbitrary",)),
        name="outproj_router",
    )(ya, yb, yc, yd, w_out_bf, h, g.reshape(1, d), b.reshape(1, d), wr_t_bf, br_col)


def _moe_dense_kernel(h_ref, c_ref, wg_ref, wu_ref, wd_ref, g_ref, b_ref, o_ref, xb_ref, acc_ref):
    e = pl.program_id(1)

    @pl.when(e == 0)
    def _():
        xb_ref[...] = h_ref[...].astype(BF16)
        acc_ref[...] = jnp.zeros_like(acc_ref)

    xb = xb_ref[...]
    hg = jnp.dot(xb, wg_ref[0], preferred_element_type=F32)
    hu = jnp.dot(xb, wu_ref[0], preferred_element_type=F32)
    lane = lax.broadcasted_iota(jnp.int32, c_ref.shape, 1)
    cw = jnp.sum(jnp.where(lane == e, c_ref[...], 0.0), axis=1, keepdims=True)
    hh = (_silu(hg) * hu * cw).astype(BF16)
    acc_ref[...] += jnp.dot(hh, wd_ref[0], preferred_element_type=F32)

    @pl.when(e == pl.num_programs(1) - 1)
    def _():
        o_ref[...] = _layer_norm(DN_ALPHA * h_ref[...] + acc_ref[...], g_ref[...], b_ref[...])


def _moe_dense(h1, comb_t, wg_bf, wu_bf, wd_bf, g, b, tm=512):
    t, d = h1.shape
    tm = min(tm, t)
    ne, _, de = wg_bf.shape
    return pl.pallas_call(
        _moe_dense_kernel,
        grid=(t // tm, ne),
        in_specs=[pl.BlockSpec((tm, d), lambda i, e: (i, 0)),
                  pl.BlockSpec((tm, N_EXPERTS), lambda i, e: (i, 0)),
                  pl.BlockSpec((1, d, de), lambda i, e: (e, 0, 0)),
                  pl.BlockSpec((1, d, de), lambda i, e: (e, 0, 0)),
                  pl.BlockSpec((1, de, d), lambda i, e: (e, 0, 0)),
                  pl.BlockSpec((1, d), lambda i, e: (0, 0)),
                  pl.BlockSpec((1, d), lambda i, e: (0, 0))],
        out_specs=pl.BlockSpec((tm, d), lambda i, e: (i, 0)),
        out_shape=jax.ShapeDtypeStruct((t, d), F32),
        scratch_shapes=[pltpu.VMEM((tm, d), BF16), pltpu.VMEM((tm, d), F32)],
        compiler_params=_cparams(("arbitrary", "arbitrary")),
        name="moe_dense",
    )(h1, comb_t, wg_bf, wu_bf, wd_bf, g.reshape(1, d), b.reshape(1, d))


def _prep_w_in(w_in_l):
    d = w_in_l.shape[0]
    ca0 = 4096
    ca1 = ca0 + GLA_RANK
    parts = [w_in_l[:, :ca0], w_in_l[:, ca1:], w_in_l[:, ca0:ca1]]
    used = ca0 + (w_in_l.shape[1] - ca1) + GLA_RANK
    parts.append(jnp.zeros((d, Z_W - used), w_in_l.dtype))
    return jnp.concatenate(parts, axis=1).astype(BF16)


def _prep_router(w_rg_l, b_rg_l, w_re_l, b_re_l):
    d = w_rg_l.shape[0]
    wr = jnp.concatenate([w_rg_l, w_re_l], axis=1)
    wr_t = jnp.zeros((R_ROWS, d), F32).at[:wr.shape[1]].set(wr.T)
    wr_hi = wr_t.astype(BF16)
    wr_lo = (wr_t - wr_hi.astype(F32)).astype(BF16)
    wr_t = jnp.stack([wr_hi, wr_lo], axis=0)
    br = jnp.concatenate([b_rg_l, b_re_l])
    br_col = jnp.zeros((R_ROWS, LANES), F32).at[:br.shape[0], :].set(br[:, None])
    return wr_t, br_col


def _layer(l, h, p):
    z = _inproj(h, _prep_w_in(p["w_in"][l]))
    ya, yb = _mixer_ab(z, p["conv_a_w"][l], p["conv_a_b"][l], p["lru_wa"][l].astype(BF16), p["lru_ba"][l],
                       p["lru_wx"][l].astype(BF16), p["lru_bx"][l], p["lru_lam"][l], p["conv_b_w"][l])
    hk = GLA_HEADS * GLA_DK
    wa2_pad = jnp.zeros((LANES, hk), F32).at[:GLA_RANK].set(p["gla_wa2"][l]).astype(BF16)
    yc = _mixer_gla(z, wa2_pad, p["gla_ba2"][l], p["gla_norm_w"][l])
    yd = _mixer_att(z, _att_bias_table(p["att_rel_bias"][l]))
    wr_t, br_col = _prep_router(p["w_rg"][l], p["b_rg"][l], p["w_re"][l], p["b_re"][l])
    h1, comb = _outproj_router(ya, yb, yc, yd, p["w_out"][l].astype(BF16), h, p["ln1_g"][l], p["ln1_b"][l],
                               wr_t, br_col)
    h2 = _moe_dense(h1, comb.T, p["w_gate"][l].astype(BF16), p["w_up"][l].astype(BF16),
                    p["w_down"][l].astype(BF16), p["ln2_g"][l], p["ln2_b"][l])
    return h2


def kernel(x, ln0_g, ln0_b, w_in, conv_a_w, conv_a_b, lru_wa, lru_ba, lru_wx, lru_bx, lru_lam, conv_b_w,
           gla_wa2, gla_ba2, gla_norm_w, att_rel_bias, w_out, ln1_g, ln1_b, w_rg, b_rg, w_re, b_re,
           w_gate, w_up, w_down, ln2_g, ln2_b):
    bsz, seq, d = x.shape
    p = dict(w_in=w_in, conv_a_w=conv_a_w, conv_a_b=conv_a_b, lru_wa=lru_wa, lru_ba=lru_ba, lru_wx=lru_wx,
             lru_bx=lru_bx, lru_lam=lru_lam, conv_b_w=conv_b_w, gla_wa2=gla_wa2, gla_ba2=gla_ba2,
             gla_norm_w=gla_norm_w, att_rel_bias=att_rel_bias, w_out=w_out, ln1_g=ln1_g, ln1_b=ln1_b,
             w_rg=w_rg, b_rg=b_rg, w_re=w_re, b_re=b_re, w_gate=w_gate, w_up=w_up, w_down=w_down,
             ln2_g=ln2_g, ln2_b=ln2_b)
    outs = []
    for bi in range(bsz):
        h = _ln_rows(x[bi], ln0_g, ln0_b)
        for l in range(DEPTH):
            h = _layer(l, h, p)
        outs.append(h)
    return jnp.stack(outs, axis=0)
```

```python
import functools
import math

import jax
import jax.numpy as jnp
import numpy as np
from jax import lax
from jax.experimental import pallas as pl
from jax.experimental.pallas import tpu as pltpu

F32 = jnp.float32
BF16 = jnp.bfloat16

D_MODEL = 2048
DEPTH = 2
CHUNK = 64
GROUP_W = 512
LRU_HEADS = 4
LRU_BLOCK = 128
LRU_CONV = 4
LRU_C = 8.0
SCONV_W = 3
GLA_HEADS = 4
GLA_DK = 64
GLA_DV = 128
GLA_RANK = 16
GLA_TAU = 16.0
ATT_HEADS = 4
ATT_HD = 128
ATT_PREV_CHUNKS = 8
REL_CLIP = 256
N_GROUPS = 4
EXPERTS_PER_GROUP = 4
N_EXPERTS = 16
D_EXPERT = 512
LN_EPS = 1e-5
RMS_EPS = 1e-6
DN_ALPHA = (2 * DEPTH) ** 0.25

LANES = 128
SUBLANES = 8
VMEM_LIMIT = 56 * 1024 * 1024

Z_W = 6144
ZB_AX, ZB_AG, ZB_BB, ZB_BC, ZB_BH, ZB_CQK, ZB_CV, ZB_CG, ZB_DQ, ZB_DK, ZB_DV = range(11)
ZB_CA128 = 44


def _cparams(sem):
    return pltpu.CompilerParams(dimension_semantics=sem, vmem_limit_bytes=VMEM_LIMIT)


def _layer_norm(x, g, b):
    mu = jnp.mean(x, axis=-1, keepdims=True)
    xc = x - mu
    var = jnp.mean(xc * xc, axis=-1, keepdims=True)
    return xc * lax.rsqrt(var + LN_EPS) * g + b


def _sigmoid(x):
    return 1.0 / (1.0 + jnp.exp(-x))


def _softplus(x):
    return jnp.maximum(x, 0.0) + jnp.log(1.0 + jnp.exp(-jnp.abs(x)))


def _silu(x):
    return x * _sigmoid(x)


def _gelu_tanh(x):
    c = math.sqrt(2.0 / math.pi)
    return 0.5 * x * (1.0 + jnp.tanh(c * (x + 0.044715 * (x * x * x))))


def _ln_kernel(x_ref, g_ref, b_ref, o_ref):
    o_ref[...] = _layer_norm(x_ref[...], g_ref[...], b_ref[...])


def _ln_rows(x, g, b, tm=512):
    t, d = x.shape
    return pl.pallas_call(
        _ln_kernel,
        grid=(t // tm,),
        in_specs=[pl.BlockSpec((tm, d), lambda i: (i, 0)),
                  pl.BlockSpec((1, d), lambda i: (0, 0)),
                  pl.BlockSpec((1, d), lambda i: (0, 0))],
        out_specs=pl.BlockSpec((tm, d), lambda i: (i, 0)),
        out_shape=jax.ShapeDtypeStruct((t, d), F32),
        compiler_params=_cparams(("arbitrary",)),
        name="ln0",
    )(x, g.reshape(1, d), b.reshape(1, d))


def _inproj_kernel(h_ref, w_ref, z_ref, hb_ref):
    @pl.when(pl.program_id(1) == 0)
    def _():
        hb_ref[...] = h_ref[...].astype(BF16)

    z_ref[...] = jnp.dot(hb_ref[...], w_ref[...], preferred_element_type=F32).astype(z_ref.dtype)


def _inproj(h, w_bf, tm=1024, tn=1024):
    t, d = h.shape
    n = w_bf.shape[1]
    tm = min(tm, t)
    return pl.pallas_call(
        _inproj_kernel,
        grid=(t // tm, n // tn),
        in_specs=[pl.BlockSpec((tm, d), lambda i, j: (i, 0)),
                  pl.BlockSpec((d, tn), lambda i, j: (0, j))],
        out_specs=pl.BlockSpec((tm, tn), lambda i, j: (i, j)),
        out_shape=jax.ShapeDtypeStruct((t, n), BF16),
        scratch_shapes=[pltpu.VMEM((tm, d), BF16)],
        compiler_params=_cparams(("arbitrary", "arbitrary")),
        name="inproj",
    )(h, w_bf)


def _ab_kernel(ax_ref, ag_ref, bb_ref, bc_ref, bh_ref,
               caw_ref, cab_ref, wa_ref, ba_ref, wx_ref, bx_ref, lam_ref, cbw_ref,
               ya_ref, yb_ref, xa_buf, xb_buf, h_st):
    tb = ax_ref.shape[0]
    pad = SUBLANES

    @pl.when(pl.program_id(0) == 0)
    def _():
        xa_buf[0:pad, :] = jnp.zeros((pad, GROUP_W), F32)
        xb_buf[0:pad, :] = jnp.zeros((pad, GROUP_W), F32)
        h_st[...] = jnp.zeros_like(h_st)

    xa_buf[pad:pad + tb, :] = ax_ref[...].astype(F32)
    u = cab_ref[...] + caw_ref[0:1, :] * xa_buf[pad - 3:pad - 3 + tb, :]
    for j in range(1, LRU_CONV):
        off = pad - (LRU_CONV - 1) + j
        u = u + caw_ref[j:j + 1, :] * xa_buf[off:off + tb, :]
    xa_buf[0:pad, :] = xa_buf[tb:tb + pad, :]

    ub = u.astype(BF16)
    r_parts, i_parts = [], []
    for hd in range(LRU_HEADS):
        sl = slice(hd * LRU_BLOCK, (hd + 1) * LRU_BLOCK)
        r_parts.append(jnp.dot(ub[:, sl], wa_ref[hd], preferred_element_type=F32))
        i_parts.append(jnp.dot(ub[:, sl], wx_ref[hd], preferred_element_type=F32))
    r = _sigmoid(jnp.concatenate(r_parts, axis=1) + ba_ref[...])
    ig = _sigmoid(jnp.concatenate(i_parts, axis=1) + bx_ref[...])

    log_a = (-LRU_C * r) * _softplus(-lam_ref[...])
    a = jnp.exp(log_a)
    th = jnp.tanh(log_a)
    mult = jnp.sqrt(-2.0 * th / (1.0 - th))
    bterm = mult * (ig * u)

    rows = lax.broadcasted_iota(jnp.int32, a.shape, 0)
    acum = a
    bcum = bterm
    d = 1
    while d < tb:
        head = rows < d
        b_sh = jnp.where(head, 0.0, pltpu.roll(bcum, d, axis=0))
        a_sh = jnp.where(head, 1.0, pltpu.roll(acum, d, axis=0))
        bcum = acum * b_sh + bcum
        acum = acum * a_sh
        d *= 2
    h = acum * h_st[0:1, :] + bcum
    h_st[0:1, :] = h[tb - 1:tb, :]
    ya_ref[...] = (h * _gelu_tanh(ag_ref[...].astype(F32))).astype(ya_ref.dtype)

    xb_buf[pad:pad + tb, :] = bc_ref[...].astype(F32) * bh_ref[...].astype(F32)
    cv = cbw_ref[0:1, :] * xb_buf[pad - 2:pad - 2 + tb, :]
    for j in range(1, SCONV_W):
        off = pad - (SCONV_W - 1) + j
        cv = cv + cbw_ref[j:j + 1, :] * xb_buf[off:off + tb, :]
    xb_buf[0:pad, :] = xb_buf[tb:tb + pad, :]
    yb_ref[...] = (bb_ref[...].astype(F32) * cv).astype(yb_ref.dtype)


def _mixer_ab(z, caw, cab, wa_bf, ba, wx_bf, bx, lam, cbw, tb=256):
    t = z.shape[0]
    tb = min(tb, t)
    w = GROUP_W

    def zspec(blk):
        return pl.BlockSpec((tb, w), lambda i, blk=blk: (i, blk))

    def full(shape):
        nd = len(shape)
        return pl.BlockSpec(shape, lambda i, nd=nd: (0,) * nd)

    return pl.pallas_call(
        _ab_kernel,
        grid=(t // tb,),
        in_specs=[zspec(ZB_AX), zspec(ZB_AG), zspec(ZB_BB), zspec(ZB_BC), zspec(ZB_BH),
                  full((LRU_CONV, w)), full((1, w)),
                  full((LRU_HEADS, LRU_BLOCK, LRU_BLOCK)), full((1, w)),
                  full((LRU_HEADS, LRU_BLOCK, LRU_BLOCK)), full((1, w)),
                  full((1, w)), full((SCONV_W, w))],
        out_specs=[pl.BlockSpec((tb, w), lambda i: (i, 0)),
                   pl.BlockSpec((tb, w), lambda i: (i, 0))],
        out_shape=[jax.ShapeDtypeStruct((t, w), BF16), jax.ShapeDtypeStruct((t, w), BF16)],
        scratch_shapes=[pltpu.VMEM((tb + SUBLANES, w), F32),
                        pltpu.VMEM((tb + SUBLANES, w), F32),
                        pltpu.VMEM((SUBLANES, w), F32)],
        compiler_params=_cparams(("arbitrary",)),
        name="mixer_ab",
    )(z, z, z, z, z, caw, cab.reshape(1, w), wa_bf, ba.reshape(1, w), wx_bf, bx.reshape(1, w),
      lam.reshape(1, w), cbw)


_NT = (((1,), (1,)), ((), ()))
_TN = (((0,), (0,)), ((), ()))


def _split_bf16(x):
    hi = x.astype(BF16)
    lo = (x - hi.astype(F32)).astype(BF16)
    return hi, lo


def _gla_kernel(qk_ref, v_ref, g_ref, ca_ref, wa2_ref, ba2_ref, nw_ref, y_ref, s_ref):
    tb = qk_ref.shape[0]
    hk = GLA_HEADS * GLA_DK
    hv = GLA_HEADS * GLA_DV
    L = CHUNK

    @pl.when(pl.program_id(0) == 0)
    def _():
        s_ref[...] = jnp.zeros_like(s_ref)

    tri = (lax.broadcasted_iota(jnp.int32, (L, L), 0) >= lax.broadcasted_iota(jnp.int32, (L, L), 1))
    tri_bf = tri.astype(BF16)
    tri4 = jnp.concatenate([tri] * GLA_HEADS, axis=0)
    lane_head = lax.broadcasted_iota(jnp.int32, (L, hk), 1) // GLA_DK
    bd_mask = (lax.broadcasted_iota(jnp.int32, (hk, hv), 0) // GLA_DK
               == lax.broadcasted_iota(jnp.int32, (hk, hv), 1) // GLA_DV)
    ones_bf = jnp.ones((L, LANES), BF16)
    scale = GLA_DK ** -0.5

    for c in range(tb // L):
        rs = slice(c * L, (c + 1) * L)
        q = qk_ref[rs, 0:hk].astype(F32) * scale
        k = qk_ref[rs, hk:2 * hk].astype(F32)
        v = v_ref[rs, :]
        x = jnp.dot(ca_ref[rs, :], wa2_ref[...], preferred_element_type=F32) + ba2_ref[...]
        la = (jnp.minimum(x, 0.0) - jnp.log(1.0 + jnp.exp(-jnp.abs(x)))) * (1.0 / GLA_TAU)
        la_hi, la_lo = _split_bf16(la)
        bcum = (jnp.dot(tri_bf, la_hi, preferred_element_type=F32)
                + jnp.dot(tri_bf, la_lo, preferred_element_type=F32))
        blast = bcum[L - 1:L, :]
        q_dec = q * jnp.exp(bcum)
        k_inv = (k * jnp.exp(-bcum)).astype(BF16)
        k_end = (k * jnp.exp(blast - bcum)).astype(BF16)

        q4 = jnp.concatenate(
            [jnp.where(lane_head == hd, q_dec, 0.0) for hd in range(GLA_HEADS)], axis=0).astype(BF16)
        sc = lax.dot_general(q4, k_inv, _NT, preferred_element_type=F32)
        sc = jnp.where(tri4, sc, 0.0).astype(BF16)
        oi = jnp.dot(sc, v, preferred_element_type=F32)
        o_intra = jnp.concatenate(
            [oi[hd * L:(hd + 1) * L, hd * GLA_DV:(hd + 1) * GLA_DV] for hd in range(GLA_HEADS)], axis=1)

        s = s_ref[...]
        o = o_intra + jnp.dot(q_dec.astype(BF16), s.astype(BF16), preferred_element_type=F32)

        kv = lax.dot_general(k_end, v, _TN, preferred_element_type=F32)
        bl_col = (lax.dot_general(la_hi, ones_bf, _TN, preferred_element_type=F32)
                  + lax.dot_general(la_lo, ones_bf, _TN, preferred_element_type=F32))
        dcol = jnp.exp(bl_col)
        s_ref[...] = s * jnp.concatenate([dcol] * (hv // LANES), axis=1) + jnp.where(bd_mask, kv, 0.0)

        parts = []
        for hd in range(GLA_HEADS):
            oh = o[:, hd * GLA_DV:(hd + 1) * GLA_DV]
            parts.append(oh * lax.rsqrt(jnp.mean(oh * oh, axis=-1, keepdims=True) + RMS_EPS))
        on = jnp.concatenate(parts, axis=1) * nw_ref[...]
        y_ref[rs, :] = (on * _silu(g_ref[rs, :].astype(F32))).astype(y_ref.dtype)


def _mixer_gla(z, wa2_pad_bf, ba2, nw, tb=256):
    t = z.shape[0]
    tb = min(tb, t)
    w = GROUP_W
    hk = GLA_HEADS * GLA_DK
    return pl.pallas_call(
        _gla_kernel,
        grid=(t // tb,),
        in_specs=[pl.BlockSpec((tb, w), lambda i: (i, ZB_CQK)),
                  pl.BlockSpec((tb, w), lambda i: (i, ZB_CV)),
                  pl.BlockSpec((tb, w), lambda i: (i, ZB_CG)),
                  pl.BlockSpec((tb, LANES), lambda i: (i, ZB_CA128)),
                  pl.BlockSpec((LANES, hk), lambda i: (0, 0)),
                  pl.BlockSpec((1, hk), lambda i: (0, 0)),
                  pl.BlockSpec((1, w), lambda i: (0, 0))],
        out_specs=pl.BlockSpec((tb, w), lambda i: (i, 0)),
        out_shape=jax.ShapeDtypeStruct((t, w), BF16),
        scratch_shapes=[pltpu.VMEM((hk, w), F32)],
        compiler_params=_cparams(("arbitrary",)),
        name="mixer_gla",
    )(z, z, z, z, wa2_pad_bf, ba2.reshape(1, hk), nw.reshape(1, w))


ATT_TQ = 512
ATT_SUB = 128
ATT_KW = ATT_SUB + ATT_PREV_CHUNKS * CHUNK


def _att_kernel(q_ref, kp_ref, kc_ref, vp_ref, vc_ref, bias_ref, y_ref, kcat, vcat):
    i = pl.program_id(0)
    kcat[0:ATT_TQ, :] = kp_ref[...]
    kcat[ATT_TQ:2 * ATT_TQ, :] = kc_ref[...]
    vcat[0:ATT_TQ, :] = vp_ref[...]
    vcat[ATT_TQ:2 * ATT_TQ, :] = vc_ref[...]
    col = lax.broadcasted_iota(jnp.int32, (ATT_SUB, ATT_KW), 1)
    scale = ATT_HD ** -0.5
    for j in range(ATT_TQ // ATT_SUB):
        r0 = j * ATT_SUB
        n_invalid = jnp.where(i == 0, ATT_TQ - r0, 0)
        dead = col < n_invalid
        outs = []
        for hd in range(ATT_HEADS):
            ls = slice(hd * ATT_HD, (hd + 1) * ATT_HD)
            qh = q_ref[r0:r0 + ATT_SUB, ls]
            kh = kcat[r0:r0 + ATT_KW, ls]
            vh = vcat[r0:r0 + ATT_KW, ls]
            s = lax.dot_general(qh, kh, _NT, preferred_element_type=F32) * scale + bias_ref[hd]
            s = jnp.where(dead, -jnp.inf, s)
            m = jnp.max(s, axis=-1, keepdims=True)
            p = jnp.exp(s - m)
            l = jnp.sum(p, axis=-1, keepdims=True)
            o = jnp.dot(p.astype(BF16), vh, preferred_element_type=F32)
            outs.append(o / l)
        y_ref[r0:r0 + ATT_SUB, :] = jnp.concatenate(outs, axis=1).astype(y_ref.dtype)


def _att_bias_table(rel_bias):
    nh = rel_bias.shape[0]
    span = ATT_PREV_CHUNKS * CHUNK
    period = ATT_KW + ATT_SUB
    edge = rel_bias[:, 2 * REL_CLIP:2 * REL_CLIP + 1]
    n_flat = span - REL_CLIP + 1
    ramp = rel_bias[:, 2 * REL_CLIP - (ATT_KW - n_flat):2 * REL_CLIP][:, ::-1]
    vec = jnp.concatenate([jnp.broadcast_to(edge, (nh, n_flat)), ramp,
                           jnp.broadcast_to(edge, (nh, period - ATT_KW))], axis=1)
    tab = jnp.tile(vec, (1, ATT_SUB))[:, :ATT_SUB * (period - 1)].reshape(nh, ATT_SUB, period - 1)[:, :, :ATT_KW]
    r = np.arange(ATT_SUB)[:, None]
    c = np.arange(ATT_KW)[None, :]
    rel = c - CHUNK * (r // CHUNK)
    valid = (rel >= 0) & (rel < (ATT_PREV_CHUNKS + 1) * CHUNK)
    return jnp.where(jnp.asarray(valid)[None], tab.astype(F32), -jnp.inf)


def _mixer_att(z, bias_tab):
    t = z.shape[0]
    w = GROUP_W
    tq = ATT_TQ

    def prev(i):
        return jnp.maximum(i - 1, 0)

    return pl.pallas_call(
        _att_kernel,
        grid=(t // tq,),
        in_specs=[pl.BlockSpec((tq, w), lambda i: (i, ZB_DQ)),
                  pl.BlockSpec((tq, w), lambda i: (prev(i), ZB_DK)),
                  pl.BlockSpec((tq, w), lambda i: (i, ZB_DK)),
                  pl.BlockSpec((tq, w), lambda i: (prev(i), ZB_DV)),
                  pl.BlockSpec((tq, w), lambda i: (i, ZB_DV)),
                  pl.BlockSpec((ATT_HEADS, ATT_SUB, ATT_KW), lambda i: (0, 0, 0))],
        out_specs=pl.BlockSpec((tq, w), lambda i: (i, 0)),
        out_shape=jax.ShapeDtypeStruct((t, w), BF16),
        scratch_shapes=[pltpu.VMEM((2 * tq, w), BF16), pltpu.VMEM((2 * tq, w), BF16)],
        compiler_params=_cparams(("arbitrary",)),
        name="mixer_att",
    )(z, z, z, z, z, bias_tab)


R_ROWS = 32
PAIRS = ((0, 1), (0, 2), (0, 3), (1, 3), (1, 2), (2, 3))
SLOT0 = (0, 0, 0, 1, 1, 3)
SLOT1 = (1, 2, 3, 3, 2, 2)
N_PAIRS = len(PAIRS)
N_CLASSES = N_GROUPS * N_PAIRS
C_ROWS = 32


def _outproj_kernel(ya_ref, yb_ref, yc_ref, yd_ref, w_ref, h_ref, g_ref, b_ref, wr_ref, br_ref,
                    h1_ref, meta_ref, counts_ref, cnt_ref):
    w = GROUP_W
    acc = jnp.dot(ya_ref[...], w_ref[0:w, :], preferred_element_type=F32)
    acc = acc + jnp.dot(yb_ref[...], w_ref[w:2 * w, :], preferred_element_type=F32)
    acc = acc + jnp.dot(yc_ref[...], w_ref[2 * w:3 * w, :], preferred_element_type=F32)
    acc = acc + jnp.dot(yd_ref[...], w_ref[3 * w:4 * w, :], preferred_element_type=F32)
    h1 = _layer_norm(DN_ALPHA * h_ref[...] + acc, g_ref[...], b_ref[...])
    h1_ref[...] = h1

    h1_hi, h1_lo = _split_bf16(h1)
    lt = (lax.dot_general(wr_ref[0], h1_hi, _NT, preferred_element_type=F32)
          + lax.dot_general(wr_ref[0], h1_lo, _NT, preferred_element_type=F32)
          + lax.dot_general(wr_ref[1], h1_hi, _NT, preferred_element_type=F32)) + br_ref[:, 0:1]
    ng = N_GROUPS
    epg = EXPERTS_PER_GROUP
    lg = lt[0:ng, :]
    gm = jnp.max(lg, axis=0, keepdims=True)
    g_val = 1.0 / jnp.sum(jnp.exp(lg - gm), axis=0, keepdims=True)
    gi = jnp.full(gm.shape, ng - 1, jnp.int32)
    for g in range(ng - 2, -1, -1):
        gi = jnp.where(lg[g:g + 1, :] == gm, g, gi)
    e = []
    for j in range(epg):
        ej = lt[ng + (ng - 1) * epg + j:ng + (ng - 1) * epg + j + 1, :]
        for g in range(ng - 2, -1, -1):
            ej = jnp.where(gi == g, lt[ng + g * epg + j:ng + g * epg + j + 1, :], ej)
        e.append(ej)
    v1 = jnp.maximum(jnp.maximum(e[0], e[1]), jnp.maximum(e[2], e[3]))
    i1 = jnp.full(v1.shape, epg - 1, jnp.int32)
    for j in range(epg - 2, -1, -1):
        i1 = jnp.where(e[j] == v1, j, i1)
    neg = -jnp.inf
    e2 = [jnp.where(i1 == j, neg, e[j]) for j in range(epg)]
    v2 = jnp.maximum(jnp.maximum(e2[0], e2[1]), jnp.maximum(e2[2], e2[3]))
    i2 = jnp.full(v1.shape, epg - 1, jnp.int32)
    for j in range(epg - 2, -1, -1):
        i2 = jnp.where(e2[j] == v2, j, i2)
    t2 = jnp.exp(v2 - v1)
    w1 = g_val / (1.0 + t2)
    w2 = g_val * t2 / (1.0 + t2)
    lo_e = jnp.minimum(i1, i2)
    hi_e = jnp.maximum(i1, i2)
    pkey = lo_e * epg + hi_e
    pair = jnp.zeros_like(pkey)
    for pi, (pa, pb) in enumerate(PAIRS):
        pair = jnp.where(pkey == pa * epg + pb, pi, pair)
    cls = gi * N_PAIRS + pair
    wloc = [jnp.where(i1 == j, w1, 0.0) + jnp.where(i2 == j, w2, 0.0) for j in range(epg)]
    ws0 = jnp.zeros_like(w1)
    ws1 = jnp.zeros_like(w1)
    for pi in range(N_PAIRS):
        ws0 = jnp.where(pair == pi, wloc[SLOT0[pi]], ws0)
        ws1 = jnp.where(pair == pi, wloc[SLOT1[pi]], ws1)

    @pl.when(pl.program_id(0) == 0)
    def _():
        cnt_ref[...] = jnp.zeros_like(cnt_ref)

    tm = lt.shape[1]
    crow = lax.broadcasted_iota(jnp.int32, (C_ROWS, tm), 0)
    ohf = jnp.where(crow == cls, 1.0, 0.0)
    upper = (lax.broadcasted_iota(jnp.int32, (tm, tm), 0)
             < lax.broadcasted_iota(jnp.int32, (tm, tm), 1)).astype(BF16)
    before = jnp.dot(ohf.astype(BF16), upper, preferred_element_type=F32)
    carry = cnt_ref[...]
    rank = jnp.sum(ohf * (before + carry[:, 0:1]), axis=0, keepdims=True)
    cnt_new = carry + jnp.sum(ohf, axis=1, keepdims=True)
    cnt_ref[...] = cnt_new
    counts_ref[...] = cnt_new
    meta_ref[...] = jnp.concatenate(
        [cls.astype(F32), ws0, ws1, rank, jnp.zeros((SUBLANES - 4, tm), F32)], axis=0)


def _outproj_router(ya, yb, yc, yd, w_out_bf, h, g, b, wr_t_bf, br_col, tm=512):
    t, d = h.shape
    w = GROUP_W
    tm = min(tm, t)

    def ys():
        return pl.BlockSpec((tm, w), lambda i: (i, 0))

    return pl.pallas_call(
        _outproj_kernel,
        grid=(t // tm,),
        in_specs=[ys(), ys(), ys(), ys(),
                  pl.BlockSpec((d, d), lambda i: (0, 0)),
                  pl.BlockSpec((tm, d), lambda i: (i, 0)),
                  pl.BlockSpec((1, d), lambda i: (0, 0)),
                  pl.BlockSpec((1, d), lambda i: (0, 0)),
                  pl.BlockSpec((2, R_ROWS, d), lambda i: (0, 0, 0)),
                  pl.BlockSpec((R_ROWS, LANES), lambda i: (0, 0))],
        out_specs=[pl.BlockSpec((tm, d), lambda i: (i, 0)),
                   pl.BlockSpec((SUBLANES, tm), lambda i: (0, i)),
                   pl.BlockSpec((C_ROWS, LANES), lambda i: (0, 0))],
        out_shape=[jax.ShapeDtypeStruct((t, d), F32),
                   jax.ShapeDtypeStruct((SUBLANES, t), F32),
                   jax.ShapeDtypeStruct((C_ROWS, LANES), F32)],
        scratch_shapes=[pltpu.VMEM((C_ROWS, LANES), F32)],
        compiler_params=_cparams(("arbitrary",)),
        name="outproj_router",
    )(ya, yb, yc, yd, w_out_bf, h, g.reshape(1, d), b.reshape(1, d), wr_t_bf, br_col)


MOVE_ROWS = 2048
MOVE_UNROLL = 8
MOVE_WINDOW = 8


def _gather_rows_kernel(idx_ref, src_ref, dst_ref, sem):
    base = pl.program_id(0) * MOVE_ROWS
    n_iter = MOVE_ROWS // MOVE_UNROLL

    def row_copy(j):
        return pltpu.make_async_copy(src_ref.at[pl.ds(idx_ref[0, 0, j], 1)],
                                     dst_ref.at[pl.ds(base + j, 1)], sem)

    def issue(it, c):
        for u in range(MOVE_UNROLL):
            row_copy(it * MOVE_UNROLL + u).start()

        @pl.when(it >= MOVE_WINDOW)
        def _():
            for u in range(MOVE_UNROLL):
                row_copy((it - MOVE_WINDOW) * MOVE_UNROLL + u).wait()
        return c

    lax.fori_loop(0, n_iter, issue, 0)

    def drain(it, c):
        for u in range(MOVE_UNROLL):
            row_copy(it * MOVE_UNROLL + u).wait()
        return c

    lax.fori_loop(n_iter - MOVE_WINDOW, n_iter, drain, 0)


def _gather_rows(src, idx):
    d = src.shape[1]
    n_out = idx.shape[0]
    steps = n_out // MOVE_ROWS
    return pl.pallas_call(
        _gather_rows_kernel,
        grid=(steps,),
        in_specs=[pl.BlockSpec((1, 1, MOVE_ROWS), lambda i: (i, 0, 0), memory_space=pltpu.SMEM),
                  pl.BlockSpec(memory_space=pl.ANY)],
        out_specs=pl.BlockSpec(memory_space=pl.ANY),
        scratch_shapes=[pltpu.SemaphoreType.DMA(())],
        out_shape=jax.ShapeDtypeStruct((n_out, d), src.dtype),
        compiler_params=_cparams(("arbitrary",)),
        name="gather_rows",
    )(idx.reshape(steps, 1, MOVE_ROWS), src)


MOE_TM = 256


def _moe_kernel(te0_ref, te1_ref, tsrc_ref, tval_ref, x_ref, w_ref, wg0_ref, wu0_ref, wd0_ref,
                wg1_ref, wu1_ref, wd1_ref, g_ref, b_ref, o_ref):
    k = pl.program_id(0)

    @pl.when(tval_ref[k] == 1)
    def _():
        x = x_ref[...]
        xb = x.astype(BF16)
        h0 = _silu(jnp.dot(xb, wg0_ref[0], preferred_element_type=F32)) * jnp.dot(
            xb, wu0_ref[0], preferred_element_type=F32)
        h1 = _silu(jnp.dot(xb, wg1_ref[0], preferred_element_type=F32)) * jnp.dot(
            xb, wu1_ref[0], preferred_element_type=F32)
        ffn = jnp.dot((h0 * w_ref[:, 0:1]).astype(BF16), wd0_ref[0], preferred_element_type=F32)
        ffn = ffn + jnp.dot((h1 * w_ref[:, 1:2]).astype(BF16), wd1_ref[0], preferred_element_type=F32)
        o_ref[...] = _layer_norm(DN_ALPHA * x + ffn, g_ref[...], b_ref[...])

    @pl.when(tval_ref[k] == 0)
    def _():
        o_ref[...] = jnp.zeros_like(o_ref)


def _moe_sorted(xs, wsort, te0, te1, tsrc, tval, wg_bf, wu_bf, wd_bf, g, b):
    n_rows, d = xs.shape
    tm = MOE_TM
    de = wg_bf.shape[2]

    def wspec(shape, slot):
        if slot == 0:
            return pl.BlockSpec(shape, lambda k, te0, te1, tsrc, tval: (te0[k], 0, 0))
        return pl.BlockSpec(shape, lambda k, te0, te1, tsrc, tval: (te1[k], 0, 0))

    return pl.pallas_call(
        _moe_kernel,
        grid_spec=pltpu.PrefetchScalarGridSpec(
            num_scalar_prefetch=4,
            grid=(n_rows // tm,),
            in_specs=[pl.BlockSpec((tm, d), lambda k, te0, te1, tsrc, tval: (tsrc[k], 0)),
                      pl.BlockSpec((tm, 2), lambda k, te0, te1, tsrc, tval: (tsrc[k], 0)),
                      wspec((1, d, de), 0), wspec((1, d, de), 0), wspec((1, de, d), 0),
                      wspec((1, d, de), 1), wspec((1, d, de), 1), wspec((1, de, d), 1),
                      pl.BlockSpec((1, d), lambda k, *_: (0, 0)),
                      pl.BlockSpec((1, d), lambda k, *_: (0, 0))],
            out_specs=pl.BlockSpec((tm, d), lambda k, *_: (k, 0))),
        out_shape=jax.ShapeDtypeStruct((n_rows, d), F32),
        compiler_params=_cparams(("arbitrary",)),
        name="moe_sorted",
    )(te0, te1, tsrc, tval, xs, wsort, wg_bf, wu_bf, wd_bf, wg_bf, wu_bf, wd_bf,
      g.reshape(1, d), b.reshape(1, d))


def _moe_plan(meta, counts, t):
    tm = MOE_TM
    n_pad = -(-(t + N_CLASSES * tm) // MOVE_ROWS) * MOVE_ROWS
    nt_max = n_pad // tm
    cls = meta[0].astype(jnp.int32)
    rank = meta[3].astype(jnp.int32)
    cnt = counts[:N_CLASSES, 0].astype(jnp.int32)
    tiles = (cnt + tm - 1) // tm
    tile_end = jnp.cumsum(tiles)
    tile_start = tile_end - tiles
    n_tiles = tile_end[-1]
    onehot = (cls[:, None] == jnp.arange(N_CLASSES)[None, :]).astype(F32)
    pos = jnp.dot(onehot, tile_start.astype(F32)).astype(jnp.int32) * tm + rank
    k = jnp.arange(nt_max)
    tval = (k < n_tiles).astype(jnp.int32)
    tsrc = jnp.minimum(k, jnp.maximum(n_tiles - 1, 0)).astype(jnp.int32)
    tcls = jnp.minimum(jnp.sum(tsrc[:, None] >= tile_end[None, :], axis=1), N_CLASSES - 1)
    grp = tcls // N_PAIRS
    pr_oh = ((tcls % N_PAIRS)[:, None] == jnp.arange(N_PAIRS)[None, :]).astype(jnp.int32)
    te0 = grp * EXPERTS_PER_GROUP + jnp.sum(pr_oh * jnp.asarray(SLOT0, jnp.int32)[None, :], axis=1)
    te1 = grp * EXPERTS_PER_GROUP + jnp.sum(pr_oh * jnp.asarray(SLOT1, jnp.int32)[None, :], axis=1)
    perm = jnp.zeros((n_pad,), jnp.int32).at[pos].set(jnp.arange(t, dtype=jnp.int32))
    wsort = jnp.zeros((n_pad, 2), F32).at[pos].set(jnp.stack([meta[1], meta[2]], axis=1))
    return pos, perm, wsort, te0.astype(jnp.int32), te1.astype(jnp.int32), tsrc, tval


def _prep_w_in(w_in_l):
    d = w_in_l.shape[0]
    ca0 = 4096
    ca1 = ca0 + GLA_RANK
    parts = [w_in_l[:, :ca0], w_in_l[:, ca1:], w_in_l[:, ca0:ca1]]
    used = ca0 + (w_in_l.shape[1] - ca1) + GLA_RANK
    parts.append(jnp.zeros((d, Z_W - used), w_in_l.dtype))
    return jnp.concatenate(parts, axis=1).astype(BF16)


def _prep_router(w_rg_l, b_rg_l, w_re_l, b_re_l):
    d = w_rg_l.shape[0]
    wr = jnp.concatenate([w_rg_l, w_re_l], axis=1)
    wr_t = jnp.zeros((R_ROWS, d), F32).at[:wr.shape[1]].set(wr.T)
    wr_hi = wr_t.astype(BF16)
    wr_lo = (wr_t - wr_hi.astype(F32)).astype(BF16)
    wr_t = jnp.stack([wr_hi, wr_lo], axis=0)
    br = jnp.concatenate([b_rg_l, b_re_l])
    br_col = jnp.zeros((R_ROWS, LANES), F32).at[:br.shape[0], :].set(br[:, None])
    return wr_t, br_col


def _layer(l, h, p):
    z = _inproj(h, _prep_w_in(p["w_in"][l]))
    ya, yb = _mixer_ab(z, p["conv_a_w"][l], p["conv_a_b"][l], p["lru_wa"][l].astype(BF16), p["lru_ba"][l],
                       p["lru_wx"][l].astype(BF16), p["lru_bx"][l], p["lru_lam"][l], p["conv_b_w"][l])
    hk = GLA_HEADS * GLA_DK
    wa2_pad = jnp.zeros((LANES, hk), F32).at[:GLA_RANK].set(p["gla_wa2"][l]).astype(BF16)
    yc = _mixer_gla(z, wa2_pad, p["gla_ba2"][l], p["gla_norm_w"][l])
    yd = _mixer_att(z, _att_bias_table(p["att_rel_bias"][l]))
    wr_t, br_col = _prep_router(p["w_rg"][l], p["b_rg"][l], p["w_re"][l], p["b_re"][l])
    h1, meta, counts = _outproj_router(ya, yb, yc, yd, p["w_out"][l].astype(BF16), h, p["ln1_g"][l],
                                       p["ln1_b"][l], wr_t, br_col)
    t = h.shape[0]
    pos, perm, wsort, te0, te1, tsrc, tval = _moe_plan(meta, counts, t)
    xs = _gather_rows(h1, perm)
    ys = _moe_sorted(xs, wsort, te0, te1, tsrc, tval, p["w_gate"][l].astype(BF16), p["w_up"][l].astype(BF16),
                     p["w_down"][l].astype(BF16), p["ln2_g"][l], p["ln2_b"][l])
    return _gather_rows(ys, pos)


def kernel(x, ln0_g, ln0_b, w_in, conv_a_w, conv_a_b, lru_wa, lru_ba, lru_wx, lru_bx, lru_lam, conv_b_w,
           gla_wa2, gla_ba2, gla_norm_w, att_rel_bias, w_out, ln1_g, ln1_b, w_rg, b_rg, w_re, b_re,
           w_gate, w_up, w_down, ln2_g, ln2_b):
    bsz, seq, d = x.shape
    p = dict(w_in=w_in, conv_a_w=conv_a_w, conv_a_b=conv_a_b, lru_wa=lru_wa, lru_ba=lru_ba, lru_wx=lru_wx,
             lru_bx=lru_bx, lru_lam=lru_lam, conv_b_w=conv_b_w, gla_wa2=gla_wa2, gla_ba2=gla_ba2,
             gla_norm_w=gla_norm_w, att_rel_bias=att_rel_bias, w_out=w_out, ln1_g=ln1_g, ln1_b=ln1_b,
             w_rg=w_rg, b_rg=b_rg, w_re=w_re, b_re=b_re, w_gate=w_gate, w_up=w_up, w_down=w_down,
             ln2_g=ln2_g, ln2_b=ln2_b)
    outs = []
    for bi in range(bsz):
        h = _ln_rows(x[bi], ln0_g, ln0_b)
        for l in range(DEPTH):
            h = _layer(l, h, p)
        outs.append(h)
    return jnp.stack(outs, axis=0)
```

```python
import functools
import math

import jax
import jax.numpy as jnp
import numpy as np
from jax import lax
from jax.experimental import pallas as pl
from jax.experimental.pallas import tpu as pltpu

F32 = jnp.float32
BF16 = jnp.bfloat16

D_MODEL = 2048
DEPTH = 2
CHUNK = 64
GROUP_W = 512
LRU_HEADS = 4
LRU_BLOCK = 128
LRU_CONV = 4
LRU_C = 8.0
SCONV_W = 3
GLA_HEADS = 4
GLA_DK = 64
GLA_DV = 128
GLA_RANK = 16
GLA_TAU = 16.0
ATT_HEADS = 4
ATT_HD = 128
ATT_PREV_CHUNKS = 8
REL_CLIP = 256
N_GROUPS = 4
EXPERTS_PER_GROUP = 4
N_EXPERTS = 16
D_EXPERT = 512
LN_EPS = 1e-5
RMS_EPS = 1e-6
DN_ALPHA = (2 * DEPTH) ** 0.25

LANES = 128
SUBLANES = 8
VMEM_LIMIT = 56 * 1024 * 1024

Z_W = 6144
ZB_AX, ZB_AG, ZB_BB, ZB_BC, ZB_BH, ZB_CQK, ZB_CV, ZB_CG, ZB_DQ, ZB_DK, ZB_DV = range(11)
ZB_CA128 = 44


def _cparams(sem):
    return pltpu.CompilerParams(dimension_semantics=sem, vmem_limit_bytes=VMEM_LIMIT)


def _layer_norm(x, g, b):
    mu = jnp.mean(x, axis=-1, keepdims=True)
    xc = x - mu
    var = jnp.mean(xc * xc, axis=-1, keepdims=True)
    return xc * lax.rsqrt(var + LN_EPS) * g + b


def _sigmoid(x):
    return 1.0 / (1.0 + jnp.exp(-x))


def _softplus(x):
    return jnp.maximum(x, 0.0) + jnp.log(1.0 + jnp.exp(-jnp.abs(x)))


def _silu(x):
    return x * _sigmoid(x)


def _gelu_tanh(x):
    c = math.sqrt(2.0 / math.pi)
    return 0.5 * x * (1.0 + jnp.tanh(c * (x + 0.044715 * (x * x * x))))


ROW_CHUNKS = D_MODEL // LANES


def _rows_chunk(ref, c, n):
    return ref[pl.ds(c, n, stride=ROW_CHUNKS), :]


def _rows_load(ref, n):
    return jnp.concatenate([_rows_chunk(ref, c, n) for c in range(ROW_CHUNKS)], axis=1)


def _rows_store(ref, val):
    n = val.shape[0]
    for c in range(ROW_CHUNKS):
        ref[pl.ds(c, n, stride=ROW_CHUNKS), :] = val[:, c * LANES:(c + 1) * LANES]


def _ln_kernel(x_ref, g_ref, b_ref, o_ref):
    _rows_store(o_ref, _layer_norm(x_ref[...], g_ref[...], b_ref[...]))


def _ln_rows(x, g, b, tm=512):
    t, d = x.shape
    return pl.pallas_call(
        _ln_kernel,
        grid=(t // tm,),
        in_specs=[pl.BlockSpec((tm, d), lambda i: (i, 0)),
                  pl.BlockSpec((1, d), lambda i: (0, 0)),
                  pl.BlockSpec((1, d), lambda i: (0, 0))],
        out_specs=pl.BlockSpec((tm * ROW_CHUNKS, LANES), lambda i: (i, 0)),
        out_shape=jax.ShapeDtypeStruct((t * ROW_CHUNKS, LANES), F32),
        compiler_params=_cparams(("arbitrary",)),
        name="ln0",
    )(x, g.reshape(1, d), b.reshape(1, d))


def _inproj_kernel(h_ref, w_ref, z_ref, hb_ref):
    @pl.when(pl.program_id(1) == 0)
    def _():
        tm = hb_ref.shape[0]
        for c in range(ROW_CHUNKS):
            hb_ref[:, c * LANES:(c + 1) * LANES] = _rows_chunk(h_ref, c, tm).astype(BF16)

    z_ref[...] = jnp.dot(hb_ref[...], w_ref[...], preferred_element_type=F32).astype(z_ref.dtype)


def _inproj(h, w_bf, tm=1024, tn=1024):
    t = h.shape[0] // ROW_CHUNKS
    d = D_MODEL
    n = w_bf.shape[1]
    tm = min(tm, t)
    return pl.pallas_call(
        _inproj_kernel,
        grid=(t // tm, n // tn),
        in_specs=[pl.BlockSpec((tm * ROW_CHUNKS, LANES), lambda i, j: (i, 0)),
                  pl.BlockSpec((d, tn), lambda i, j: (0, j))],
        out_specs=pl.BlockSpec((tm, tn), lambda i, j: (i, j)),
        out_shape=jax.ShapeDtypeStruct((t, n), BF16),
        scratch_shapes=[pltpu.VMEM((tm, d), BF16)],
        compiler_params=_cparams(("arbitrary", "arbitrary")),
        name="inproj",
    )(h, w_bf)


def _ab_kernel(ax_ref, ag_ref, bb_ref, bc_ref, bh_ref,
               caw_ref, cab_ref, wa_ref, ba_ref, wx_ref, bx_ref, lam_ref, cbw_ref,
               ya_ref, yb_ref, xa_buf, xb_buf, h_st):
    tb = ax_ref.shape[0]
    pad = SUBLANES

    @pl.when(pl.program_id(0) == 0)
    def _():
        xa_buf[0:pad, :] = jnp.zeros((pad, GROUP_W), F32)
        xb_buf[0:pad, :] = jnp.zeros((pad, GROUP_W), F32)
        h_st[...] = jnp.zeros_like(h_st)

    xa_buf[pad:pad + tb, :] = ax_ref[...].astype(F32)
    u = cab_ref[...] + caw_ref[0:1, :] * xa_buf[pad - 3:pad - 3 + tb, :]
    for j in range(1, LRU_CONV):
        off = pad - (LRU_CONV - 1) + j
        u = u + caw_ref[j:j + 1, :] * xa_buf[off:off + tb, :]
    xa_buf[0:pad, :] = xa_buf[tb:tb + pad, :]

    ub = u.astype(BF16)
    r_parts, i_parts = [], []
    for hd in range(LRU_HEADS):
        sl = slice(hd * LRU_BLOCK, (hd + 1) * LRU_BLOCK)
        r_parts.append(jnp.dot(ub[:, sl], wa_ref[hd], preferred_element_type=F32))
        i_parts.append(jnp.dot(ub[:, sl], wx_ref[hd], preferred_element_type=F32))
    r = _sigmoid(jnp.concatenate(r_parts, axis=1) + ba_ref[...])
    ig = _sigmoid(jnp.concatenate(i_parts, axis=1) + bx_ref[...])

    log_a = (-LRU_C * r) * _softplus(-lam_ref[...])
    a = jnp.exp(log_a)
    th = jnp.tanh(log_a)
    mult = jnp.sqrt(-2.0 * th / (1.0 - th))
    bterm = mult * (ig * u)

    rows = lax.broadcasted_iota(jnp.int32, a.shape, 0)
    acum = a
    bcum = bterm
    d = 1
    while d < tb:
        head = rows < d
        b_sh = jnp.where(head, 0.0, pltpu.roll(bcum, d, axis=0))
        a_sh = jnp.where(head, 1.0, pltpu.roll(acum, d, axis=0))
        bcum = acum * b_sh + bcum
        acum = acum * a_sh
        d *= 2
    h = acum * h_st[0:1, :] + bcum
    h_st[0:1, :] = h[tb - 1:tb, :]
    ya_ref[...] = (h * _gelu_tanh(ag_ref[...].astype(F32))).astype(ya_ref.dtype)

    xb_buf[pad:pad + tb, :] = bc_ref[...].astype(F32) * bh_ref[...].astype(F32)
    cv = cbw_ref[0:1, :] * xb_buf[pad - 2:pad - 2 + tb, :]
    for j in range(1, SCONV_W):
        off = pad - (SCONV_W - 1) + j
        cv = cv + cbw_ref[j:j + 1, :] * xb_buf[off:off + tb, :]
    xb_buf[0:pad, :] = xb_buf[tb:tb + pad, :]
    yb_ref[...] = (bb_ref[...].astype(F32) * cv).astype(yb_ref.dtype)


def _mixer_ab(z, caw, cab, wa_bf, ba, wx_bf, bx, lam, cbw, tb=256):
    t = z.shape[0]
    tb = min(tb, t)
    w = GROUP_W

    def zspec(blk):
        return pl.BlockSpec((tb, w), lambda i, blk=blk: (i, blk))

    def full(shape):
        nd = len(shape)
        return pl.BlockSpec(shape, lambda i, nd=nd: (0,) * nd)

    return pl.pallas_call(
        _ab_kernel,
        grid=(t // tb,),
        in_specs=[zspec(ZB_AX), zspec(ZB_AG), zspec(ZB_BB), zspec(ZB_BC), zspec(ZB_BH),
                  full((LRU_CONV, w)), full((1, w)),
                  full((LRU_HEADS, LRU_BLOCK, LRU_BLOCK)), full((1, w)),
                  full((LRU_HEADS, LRU_BLOCK, LRU_BLOCK)), full((1, w)),
                  full((1, w)), full((SCONV_W, w))],
        out_specs=[pl.BlockSpec((tb, w), lambda i: (i, 0)),
                   pl.BlockSpec((tb, w), lambda i: (i, 0))],
        out_shape=[jax.ShapeDtypeStruct((t, w), BF16), jax.ShapeDtypeStruct((t, w), BF16)],
        scratch_shapes=[pltpu.VMEM((tb + SUBLANES, w), F32),
                        pltpu.VMEM((tb + SUBLANES, w), F32),
                        pltpu.VMEM((SUBLANES, w), F32)],
        compiler_params=_cparams(("arbitrary",)),
        name="mixer_ab",
    )(z, z, z, z, z, caw, cab.reshape(1, w), wa_bf, ba.reshape(1, w), wx_bf, bx.reshape(1, w),
      lam.reshape(1, w), cbw)


_NT = (((1,), (1,)), ((), ()))
_TN = (((0,), (0,)), ((), ()))


def _split_bf16(x):
    hi = x.astype(BF16)
    lo = (x - hi.astype(F32)).astype(BF16)
    return hi, lo


def _gla_kernel(qk_ref, v_ref, g_ref, ca_ref, wa2_ref, ba2_ref, nw_ref, y_ref, s_ref):
    tb = qk_ref.shape[0]
    hk = GLA_HEADS * GLA_DK
    hv = GLA_HEADS * GLA_DV
    L = CHUNK

    @pl.when(pl.program_id(0) == 0)
    def _():
        s_ref[...] = jnp.zeros_like(s_ref)

    tri = (lax.broadcasted_iota(jnp.int32, (L, L), 0) >= lax.broadcasted_iota(jnp.int32, (L, L), 1))
    tri_bf = tri.astype(BF16)
    tri4 = jnp.concatenate([tri] * GLA_HEADS, axis=0)
    lane_head = lax.broadcasted_iota(jnp.int32, (L, hk), 1) // GLA_DK
    bd_mask = (lax.broadcasted_iota(jnp.int32, (hk, hv), 0) // GLA_DK
               == lax.broadcasted_iota(jnp.int32, (hk, hv), 1) // GLA_DV)
    ones_bf = jnp.ones((L, LANES), BF16)
    scale = GLA_DK ** -0.5

    for c in range(tb // L):
        rs = slice(c * L, (c + 1) * L)
        q = qk_ref[rs, 0:hk].astype(F32) * scale
        k = qk_ref[rs, hk:2 * hk].astype(F32)
        v = v_ref[rs, :]
        x = jnp.dot(ca_ref[rs, :], wa2_ref[...], preferred_element_type=F32) + ba2_ref[...]
        la = (jnp.minimum(x, 0.0) - jnp.log(1.0 + jnp.exp(-jnp.abs(x)))) * (1.0 / GLA_TAU)
        la_hi, la_lo = _split_bf16(la)
        bcum = (jnp.dot(tri_bf, la_hi, preferred_element_type=F32)
                + jnp.dot(tri_bf, la_lo, preferred_element_type=F32))
        blast = bcum[L - 1:L, :]
        q_dec = q * jnp.exp(bcum)
        k_inv = (k * jnp.exp(-bcum)).astype(BF16)
        k_end = (k * jnp.exp(blast - bcum)).astype(BF16)

        q4 = jnp.concatenate(
            [jnp.where(lane_head == hd, q_dec, 0.0) for hd in range(GLA_HEADS)], axis=0).astype(BF16)
        sc = lax.dot_general(q4, k_inv, _NT, preferred_element_type=F32)
        sc = jnp.where(tri4, sc, 0.0).astype(BF16)
        oi = jnp.dot(sc, v, preferred_element_type=F32)
        o_intra = jnp.concatenate(
            [oi[hd * L:(hd + 1) * L, hd * GLA_DV:(hd + 1) * GLA_DV] for hd in range(GLA_HEADS)], axis=1)

        s = s_ref[...]
        o = o_intra + jnp.dot(q_dec.astype(BF16), s.astype(BF16), preferred_element_type=F32)

        kv = lax.dot_general(k_end, v, _TN, preferred_element_type=F32)
        bl_col = (lax.dot_general(la_hi, ones_bf, _TN, preferred_element_type=F32)
                  + lax.dot_general(la_lo, ones_bf, _TN, preferred_element_type=F32))
        dcol = jnp.exp(bl_col)
        s_ref[...] = s * jnp.concatenate([dcol] * (hv // LANES), axis=1) + jnp.where(bd_mask, kv, 0.0)

        parts = []
        for hd in range(GLA_HEADS):
            oh = o[:, hd * GLA_DV:(hd + 1) * GLA_DV]
            parts.append(oh * lax.rsqrt(jnp.mean(oh * oh, axis=-1, keepdims=True) + RMS_EPS))
        on = jnp.concatenate(parts, axis=1) * nw_ref[...]
        y_ref[rs, :] = (on * _silu(g_ref[rs, :].astype(F32))).astype(y_ref.dtype)


def _mixer_gla(z, wa2_pad_bf, ba2, nw, tb=256):
    t = z.shape[0]
    tb = min(tb, t)
    w = GROUP_W
    hk = GLA_HEADS * GLA_DK
    return pl.pallas_call(
        _gla_kernel,
        grid=(t // tb,),
        in_specs=[pl.BlockSpec((tb, w), lambda i: (i, ZB_CQK)),
                  pl.BlockSpec((tb, w), lambda i: (i, ZB_CV)),
                  pl.BlockSpec((tb, w), lambda i: (i, ZB_CG)),
                  pl.BlockSpec((tb, LANES), lambda i: (i, ZB_CA128)),
                  pl.BlockSpec((LANES, hk), lambda i: (0, 0)),
                  pl.BlockSpec((1, hk), lambda i: (0, 0)),
                  pl.BlockSpec((1, w), lambda i: (0, 0))],
        out_specs=pl.BlockSpec((tb, w), lambda i: (i, 0)),
        out_shape=jax.ShapeDtypeStruct((t, w), BF16),
        scratch_shapes=[pltpu.VMEM((hk, w), F32)],
        compiler_params=_cparams(("arbitrary",)),
        name="mixer_gla",
    )(z, z, z, z, wa2_pad_bf, ba2.reshape(1, hk), nw.reshape(1, w))


ATT_TQ = 512
ATT_SUB = 128
ATT_KW = ATT_SUB + ATT_PREV_CHUNKS * CHUNK


def _att_kernel(q_ref, kp_ref, kc_ref, vp_ref, vc_ref, bias_ref, y_ref, kcat, vcat):
    i = pl.program_id(0)
    kcat[0:ATT_TQ, :] = kp_ref[...]
    kcat[ATT_TQ:2 * ATT_TQ, :] = kc_ref[...]
    vcat[0:ATT_TQ, :] = vp_ref[...]
    vcat[ATT_TQ:2 * ATT_TQ, :] = vc_ref[...]
    col = lax.broadcasted_iota(jnp.int32, (ATT_SUB, ATT_KW), 1)
    scale = ATT_HD ** -0.5
    for j in range(ATT_TQ // ATT_SUB):
        r0 = j * ATT_SUB
        n_invalid = jnp.where(i == 0, ATT_TQ - r0, 0)
        dead = col < n_invalid
        outs = []
        for hd in range(ATT_HEADS):
            ls = slice(hd * ATT_HD, (hd + 1) * ATT_HD)
            qh = q_ref[r0:r0 + ATT_SUB, ls]
            kh = kcat[r0:r0 + ATT_KW, ls]
            vh = vcat[r0:r0 + ATT_KW, ls]
            s = lax.dot_general(qh, kh, _NT, preferred_element_type=F32) * scale + bias_ref[hd]
            s = jnp.where(dead, -jnp.inf, s)
            m = jnp.max(s, axis=-1, keepdims=True)
            p = jnp.exp(s - m)
            l = jnp.sum(p, axis=-1, keepdims=True)
            o = jnp.dot(p.astype(BF16), vh, preferred_element_type=F32)
            outs.append(o / l)
        y_ref[r0:r0 + ATT_SUB, :] = jnp.concatenate(outs, axis=1).astype(y_ref.dtype)


def _att_bias_table(rel_bias):
    nh = rel_bias.shape[0]
    span = ATT_PREV_CHUNKS * CHUNK
    period = ATT_KW + ATT_SUB
    edge = rel_bias[:, 2 * REL_CLIP:2 * REL_CLIP + 1]
    n_flat = span - REL_CLIP + 1
    ramp = rel_bias[:, 2 * REL_CLIP - (ATT_KW - n_flat):2 * REL_CLIP][:, ::-1]
    vec = jnp.concatenate([jnp.broadcast_to(edge, (nh, n_flat)), ramp,
                           jnp.broadcast_to(edge, (nh, period - ATT_KW))], axis=1)
    tab = jnp.tile(vec, (1, ATT_SUB))[:, :ATT_SUB * (period - 1)].reshape(nh, ATT_SUB, period - 1)[:, :, :ATT_KW]
    r = np.arange(ATT_SUB)[:, None]
    c = np.arange(ATT_KW)[None, :]
    rel = c - CHUNK * (r // CHUNK)
    valid = (rel >= 0) & (rel < (ATT_PREV_CHUNKS + 1) * CHUNK)
    return jnp.where(jnp.asarray(valid)[None], tab.astype(F32), -jnp.inf)


def _mixer_att(z, bias_tab):
    t = z.shape[0]
    w = GROUP_W
    tq = ATT_TQ

    def prev(i):
        return jnp.maximum(i - 1, 0)

    return pl.pallas_call(
        _att_kernel,
        grid=(t // tq,),
        in_specs=[pl.BlockSpec((tq, w), lambda i: (i, ZB_DQ)),
                  pl.BlockSpec((tq, w), lambda i: (prev(i), ZB_DK)),
                  pl.BlockSpec((tq, w), lambda i: (i, ZB_DK)),
                  pl.BlockSpec((tq, w), lambda i: (prev(i), ZB_DV)),
                  pl.BlockSpec((tq, w), lambda i: (i, ZB_DV)),
                  pl.BlockSpec((ATT_HEADS, ATT_SUB, ATT_KW), lambda i: (0, 0, 0))],
        out_specs=pl.BlockSpec((tq, w), lambda i: (i, 0)),
        out_shape=jax.ShapeDtypeStruct((t, w), BF16),
        scratch_shapes=[pltpu.VMEM((2 * tq, w), BF16), pltpu.VMEM((2 * tq, w), BF16)],
        compiler_params=_cparams(("arbitrary",)),
        name="mixer_att",
    )(z, z, z, z, z, bias_tab)


R_ROWS = 32
PAIRS = ((0, 1), (0, 2), (0, 3), (1, 3), (1, 2), (2, 3))
SLOT0 = (0, 0, 0, 1, 1, 3)
SLOT1 = (1, 2, 3, 3, 2, 2)
N_PAIRS = len(PAIRS)
N_CLASSES = N_GROUPS * N_PAIRS
C_ROWS = 32


def _outproj_kernel(ya_ref, yb_ref, yc_ref, yd_ref, w_ref, h_ref, g_ref, b_ref, wr_ref, br_ref,
                    h1_ref, meta_ref, counts_ref, cnt_ref):
    w = GROUP_W
    acc = jnp.dot(ya_ref[...], w_ref[0:w, :], preferred_element_type=F32)
    acc = acc + jnp.dot(yb_ref[...], w_ref[w:2 * w, :], preferred_element_type=F32)
    acc = acc + jnp.dot(yc_ref[...], w_ref[2 * w:3 * w, :], preferred_element_type=F32)
    acc = acc + jnp.dot(yd_ref[...], w_ref[3 * w:4 * w, :], preferred_element_type=F32)
    tm = ya_ref.shape[0]
    h1 = _layer_norm(DN_ALPHA * _rows_load(h_ref, tm) + acc, g_ref[...], b_ref[...])
    _rows_store(h1_ref, h1)

    h1_hi, h1_lo = _split_bf16(h1)
    lt = (lax.dot_general(wr_ref[0], h1_hi, _NT, preferred_element_type=F32)
          + lax.dot_general(wr_ref[0], h1_lo, _NT, preferred_element_type=F32)
          + lax.dot_general(wr_ref[1], h1_hi, _NT, preferred_element_type=F32)) + br_ref[:, 0:1]
    ng = N_GROUPS
    epg = EXPERTS_PER_GROUP
    lg = lt[0:ng, :]
    gm = jnp.max(lg, axis=0, keepdims=True)
    g_val = 1.0 / jnp.sum(jnp.exp(lg - gm), axis=0, keepdims=True)
    gi = jnp.full(gm.shape, ng - 1, jnp.int32)
    for g in range(ng - 2, -1, -1):
        gi = jnp.where(lg[g:g + 1, :] == gm, g, gi)
    e = []
    for j in range(epg):
        ej = lt[ng + (ng - 1) * epg + j:ng + (ng - 1) * epg + j + 1, :]
        for g in range(ng - 2, -1, -1):
            ej = jnp.where(gi == g, lt[ng + g * epg + j:ng + g * epg + j + 1, :], ej)
        e.append(ej)
    v1 = jnp.maximum(jnp.maximum(e[0], e[1]), jnp.maximum(e[2], e[3]))
    i1 = jnp.full(v1.shape, epg - 1, jnp.int32)
    for j in range(epg - 2, -1, -1):
        i1 = jnp.where(e[j] == v1, j, i1)
    neg = -jnp.inf
    e2 = [jnp.where(i1 == j, neg, e[j]) for j in range(epg)]
    v2 = jnp.maximum(jnp.maximum(e2[0], e2[1]), jnp.maximum(e2[2], e2[3]))
    i2 = jnp.full(v1.shape, epg - 1, jnp.int32)
    for j in range(epg - 2, -1, -1):
        i2 = jnp.where(e2[j] == v2, j, i2)
    t2 = jnp.exp(v2 - v1)
    w1 = g_val / (1.0 + t2)
    w2 = g_val * t2 / (1.0 + t2)
    lo_e = jnp.minimum(i1, i2)
    hi_e = jnp.maximum(i1, i2)
    pkey = lo_e * epg + hi_e
    pair = jnp.zeros_like(pkey)
    for pi, (pa, pb) in enumerate(PAIRS):
        pair = jnp.where(pkey == pa * epg + pb, pi, pair)
    cls = gi * N_PAIRS + pair
    wloc = [jnp.where(i1 == j, w1, 0.0) + jnp.where(i2 == j, w2, 0.0) for j in range(epg)]
    ws0 = jnp.zeros_like(w1)
    ws1 = jnp.zeros_like(w1)
    for pi in range(N_PAIRS):
        ws0 = jnp.where(pair == pi, wloc[SLOT0[pi]], ws0)
        ws1 = jnp.where(pair == pi, wloc[SLOT1[pi]], ws1)

    @pl.when(pl.program_id(0) == 0)
    def _():
        cnt_ref[...] = jnp.zeros_like(cnt_ref)

    tm = lt.shape[1]
    crow = lax.broadcasted_iota(jnp.int32, (C_ROWS, tm), 0)
    ohf = jnp.where(crow == cls, 1.0, 0.0)
    upper = (lax.broadcasted_iota(jnp.int32, (tm, tm), 0)
             < lax.broadcasted_iota(jnp.int32, (tm, tm), 1)).astype(BF16)
    before = jnp.dot(ohf.astype(BF16), upper, preferred_element_type=F32)
    carry = cnt_ref[...]
    rank = jnp.sum(ohf * (before + carry[:, 0:1]), axis=0, keepdims=True)
    cnt_new = carry + jnp.sum(ohf, axis=1, keepdims=True)
    cnt_ref[...] = cnt_new
    counts_ref[...] = cnt_new
    meta_ref[...] = jnp.concatenate(
        [cls.astype(F32), ws0, ws1, rank, jnp.zeros((SUBLANES - 4, tm), F32)], axis=0)


def _outproj_router(ya, yb, yc, yd, w_out_bf, h, g, b, wr_t_bf, br_col, tm=512):
    t = h.shape[0] // ROW_CHUNKS
    d = D_MODEL
    w = GROUP_W
    tm = min(tm, t)

    def ys():
        return pl.BlockSpec((tm, w), lambda i: (i, 0))

    def rows():
        return pl.BlockSpec((tm * ROW_CHUNKS, LANES), lambda i: (i, 0))

    return pl.pallas_call(
        _outproj_kernel,
        grid=(t // tm,),
        in_specs=[ys(), ys(), ys(), ys(),
                  pl.BlockSpec((d, d), lambda i: (0, 0)),
                  rows(),
                  pl.BlockSpec((1, d), lambda i: (0, 0)),
                  pl.BlockSpec((1, d), lambda i: (0, 0)),
                  pl.BlockSpec((2, R_ROWS, d), lambda i: (0, 0, 0)),
                  pl.BlockSpec((R_ROWS, LANES), lambda i: (0, 0))],
        out_specs=[rows(),
                   pl.BlockSpec((SUBLANES, tm), lambda i: (0, i)),
                   pl.BlockSpec((C_ROWS, LANES), lambda i: (0, 0))],
        out_shape=[jax.ShapeDtypeStruct((t * ROW_CHUNKS, LANES), F32),
                   jax.ShapeDtypeStruct((SUBLANES, t), F32),
                   jax.ShapeDtypeStruct((C_ROWS, LANES), F32)],
        scratch_shapes=[pltpu.VMEM((C_ROWS, LANES), F32)],
        compiler_params=_cparams(("arbitrary",)),
        name="outproj_router",
    )(ya, yb, yc, yd, w_out_bf, h, g.reshape(1, d), b.reshape(1, d), wr_t_bf, br_col)


MOE_TM = 256
DMA_UNROLL = 8


def _moe_kernel(te0_ref, te1_ref, nreal_ref, nt_ref,
                tok_ref, tok_next_ref, w_ref, wg0_ref, wu0_ref, wd0_ref, wg1_ref, wu1_ref, wd1_ref,
                g_ref, b_ref, hin_ref, hout_ref, xg, yo, gsem, ssem):
    k = pl.program_id(0)
    nt = nt_ref[0]
    tm = MOE_TM
    slot = lax.rem(k, 2)
    rc = ROW_CHUNKS

    def start_gather(idx_ref, s):
        def body(it, c):
            for u in range(DMA_UNROLL):
                r = it * DMA_UNROLL + u
                src = hin_ref.at[pl.ds(pl.multiple_of(idx_ref[0, 0, r] * rc, rc), rc)]
                pltpu.make_async_copy(src, xg.at[s, pl.ds(pl.multiple_of(r * rc, rc), rc)], gsem.at[s]).start()
            return c
        lax.fori_loop(0, tm // DMA_UNROLL, body, 0)

    def wait_gather(s):
        pltpu.make_async_copy(hin_ref.at[pl.ds(0, tm * rc)], xg.at[s], gsem.at[s]).wait()

    def scatter_copy(s, r, tok):
        return pltpu.make_async_copy(yo.at[s, pl.ds(pl.multiple_of(r * rc, rc), rc)],
                                     hout_ref.at[pl.ds(pl.multiple_of(tok * rc, rc), rc)], ssem.at[s])

    def start_scatter(s, n):
        def body(r, c):
            scatter_copy(s, r, tok_ref[0, 0, r]).start()
            return c
        lax.fori_loop(0, n, body, 0)

    def wait_scatter(s, n):
        def body(r, c):
            scatter_copy(s, r, 0).wait()
            return c
        lax.fori_loop(0, n, body, 0)

    @pl.when(k == 0)
    def _():
        start_gather(tok_ref, 0)

    @pl.when(k + 1 < nt)
    def _():
        start_gather(tok_next_ref, 1 - slot)

    @pl.when(k < nt)
    def _():
        wait_gather(slot)
        x = _rows_load(xg.at[slot], tm)
        xb = x.astype(BF16)
        h0 = _silu(jnp.dot(xb, wg0_ref[0], preferred_element_type=F32)) * jnp.dot(
            xb, wu0_ref[0], preferred_element_type=F32)
        h1 = _silu(jnp.dot(xb, wg1_ref[0], preferred_element_type=F32)) * jnp.dot(
            xb, wu1_ref[0], preferred_element_type=F32)
        ffn = jnp.dot((h0 * w_ref[:, 0:1]).astype(BF16), wd0_ref[0], preferred_element_type=F32)
        ffn = ffn + jnp.dot((h1 * w_ref[:, 1:2]).astype(BF16), wd1_ref[0], preferred_element_type=F32)
        y = _layer_norm(DN_ALPHA * x + ffn, g_ref[...], b_ref[...])

        @pl.when(k >= 2)
        def _():
            wait_scatter(slot, nreal_ref[k - 2])

        _rows_store(yo.at[slot], y)
        start_scatter(slot, nreal_ref[k])

        @pl.when(k == nt - 1)
        def _():
            @pl.when(k >= 1)
            def _():
                wait_scatter(1 - slot, nreal_ref[k - 1])
            wait_scatter(slot, nreal_ref[k])


def _moe_sorted(h1, perm, wsort, te0, te1, nreal, nt, wg_bf, wu_bf, wd_bf, g, b):
    d = D_MODEL
    tm = MOE_TM
    de = wg_bf.shape[2]
    nt_max = perm.shape[0] // tm

    def wspec(shape, slot):
        if slot == 0:
            return pl.BlockSpec(shape, lambda k, te0, te1, nreal, nt: (te0[k], 0, 0))
        return pl.BlockSpec(shape, lambda k, te0, te1, nreal, nt: (te1[k], 0, 0))

    def cur(k, te0, te1, nreal, nt):
        return jnp.minimum(k, nt[0] - 1)

    def nxt(k, te0, te1, nreal, nt):
        return jnp.minimum(k + 1, nt[0] - 1)

    return pl.pallas_call(
        _moe_kernel,
        grid_spec=pltpu.PrefetchScalarGridSpec(
            num_scalar_prefetch=4,
            grid=(nt_max,),
            in_specs=[pl.BlockSpec((1, 1, tm), lambda *a: (cur(*a), 0, 0), memory_space=pltpu.SMEM),
                      pl.BlockSpec((1, 1, tm), lambda *a: (nxt(*a), 0, 0), memory_space=pltpu.SMEM),
                      pl.BlockSpec((tm, 2), lambda *a: (cur(*a), 0)),
                      wspec((1, d, de), 0), wspec((1, d, de), 0), wspec((1, de, d), 0),
                      wspec((1, d, de), 1), wspec((1, d, de), 1), wspec((1, de, d), 1),
                      pl.BlockSpec((1, d), lambda k, *_: (0, 0)),
                      pl.BlockSpec((1, d), lambda k, *_: (0, 0)),
                      pl.BlockSpec(memory_space=pl.ANY)],
            out_specs=pl.BlockSpec(memory_space=pl.ANY),
            scratch_shapes=[pltpu.VMEM((2, tm * ROW_CHUNKS, LANES), F32),
                            pltpu.VMEM((2, tm * ROW_CHUNKS, LANES), F32),
                            pltpu.SemaphoreType.DMA((2,)),
                            pltpu.SemaphoreType.DMA((2,))]),
        out_shape=jax.ShapeDtypeStruct(h1.shape, F32),
        compiler_params=_cparams(("arbitrary",)),
        name="moe_sorted",
    )(te0, te1, nreal, nt, perm.reshape(nt_max, 1, tm), perm.reshape(nt_max, 1, tm), wsort,
      wg_bf, wu_bf, wd_bf, wg_bf, wu_bf, wd_bf, g.reshape(1, d), b.reshape(1, d), h1)


def _moe_plan(meta, counts, t):
    tm = MOE_TM
    nt_max = t // tm + N_CLASSES
    n_pad = nt_max * tm
    cls = meta[0].astype(jnp.int32)
    rank = meta[3].astype(jnp.int32)
    cnt = counts[:N_CLASSES, 0].astype(jnp.int32)
    tiles = (cnt + tm - 1) // tm
    tile_end = jnp.cumsum(tiles)
    tile_start = tile_end - tiles
    n_tiles = tile_end[-1]
    onehot = (cls[:, None] == jnp.arange(N_CLASSES)[None, :]).astype(F32)
    pos = jnp.dot(onehot, tile_start.astype(F32)).astype(jnp.int32) * tm + rank
    k = jnp.arange(nt_max)
    ksrc = jnp.minimum(k, jnp.maximum(n_tiles - 1, 0))
    tcls_oh = ((jnp.sum(ksrc[:, None] >= tile_end[None, :], axis=1))[:, None]
               == jnp.arange(N_CLASSES)[None, :]).astype(jnp.int32)
    tcls = jnp.sum(tcls_oh * jnp.arange(N_CLASSES)[None, :], axis=1)
    grp = tcls // N_PAIRS
    pr_oh = ((tcls % N_PAIRS)[:, None] == jnp.arange(N_PAIRS)[None, :]).astype(jnp.int32)
    te0 = grp * EXPERTS_PER_GROUP + jnp.sum(pr_oh * jnp.asarray(SLOT0, jnp.int32)[None, :], axis=1)
    te1 = grp * EXPERTS_PER_GROUP + jnp.sum(pr_oh * jnp.asarray(SLOT1, jnp.int32)[None, :], axis=1)
    c_cnt = jnp.sum(tcls_oh * cnt[None, :], axis=1)
    c_start = jnp.sum(tcls_oh * tile_start[None, :], axis=1)
    nreal = jnp.where(k < n_tiles, jnp.clip(c_cnt - (k - c_start) * tm, 0, tm), 0)
    perm = jnp.zeros((n_pad,), jnp.int32).at[pos].set(jnp.arange(t, dtype=jnp.int32))
    wsort = jnp.zeros((n_pad, 2), F32).at[pos].set(jnp.stack([meta[1], meta[2]], axis=1))
    return (perm, wsort, te0.astype(jnp.int32), te1.astype(jnp.int32), nreal.astype(jnp.int32),
            n_tiles.astype(jnp.int32).reshape(1))


def _prep_w_in(w_in_l):
    d = w_in_l.shape[0]
    ca0 = 4096
    ca1 = ca0 + GLA_RANK
    parts = [w_in_l[:, :ca0], w_in_l[:, ca1:], w_in_l[:, ca0:ca1]]
    used = ca0 + (w_in_l.shape[1] - ca1) + GLA_RANK
    parts.append(jnp.zeros((d, Z_W - used), w_in_l.dtype))
    return jnp.concatenate(parts, axis=1).astype(BF16)


def _prep_router(w_rg_l, b_rg_l, w_re_l, b_re_l):
    d = w_rg_l.shape[0]
    wr = jnp.concatenate([w_rg_l, w_re_l], axis=1)
    wr_t = jnp.zeros((R_ROWS, d), F32).at[:wr.shape[1]].set(wr.T)
    wr_hi = wr_t.astype(BF16)
    wr_lo = (wr_t - wr_hi.astype(F32)).astype(BF16)
    wr_t = jnp.stack([wr_hi, wr_lo], axis=0)
    br = jnp.concatenate([b_rg_l, b_re_l])
    br_col = jnp.zeros((R_ROWS, LANES), F32).at[:br.shape[0], :].set(br[:, None])
    return wr_t, br_col


def _layer(l, h, p):
    z = _inproj(h, _prep_w_in(p["w_in"][l]))
    ya, yb = _mixer_ab(z, p["conv_a_w"][l], p["conv_a_b"][l], p["lru_wa"][l].astype(BF16), p["lru_ba"][l],
                       p["lru_wx"][l].astype(BF16), p["lru_bx"][l], p["lru_lam"][l], p["conv_b_w"][l])
    hk = GLA_HEADS * GLA_DK
    wa2_pad = jnp.zeros((LANES, hk), F32).at[:GLA_RANK].set(p["gla_wa2"][l]).astype(BF16)
    yc = _mixer_gla(z, wa2_pad, p["gla_ba2"][l], p["gla_norm_w"][l])
    yd = _mixer_att(z, _att_bias_table(p["att_rel_bias"][l]))
    wr_t, br_col = _prep_router(p["w_rg"][l], p["b_rg"][l], p["w_re"][l], p["b_re"][l])
    h1, meta, counts = _outproj_router(ya, yb, yc, yd, p["w_out"][l].astype(BF16), h, p["ln1_g"][l],
                                       p["ln1_b"][l], wr_t, br_col)
    t = h.shape[0] // ROW_CHUNKS
    perm, wsort, te0, te1, nreal, nt = _moe_plan(meta, counts, t)
    return _moe_sorted(h1, perm, wsort, te0, te1, nreal, nt, p["w_gate"][l].astype(BF16),
                       p["w_up"][l].astype(BF16), p["w_down"][l].astype(BF16), p["ln2_g"][l], p["ln2_b"][l])


def kernel(x, ln0_g, ln0_b, w_in, conv_a_w, conv_a_b, lru_wa, lru_ba, lru_wx, lru_bx, lru_lam, conv_b_w,
           gla_wa2, gla_ba2, gla_norm_w, att_rel_bias, w_out, ln1_g, ln1_b, w_rg, b_rg, w_re, b_re,
           w_gate, w_up, w_down, ln2_g, ln2_b):
    bsz, seq, d = x.shape
    p = dict(w_in=w_in, conv_a_w=conv_a_w, conv_a_b=conv_a_b, lru_wa=lru_wa, lru_ba=lru_ba, lru_wx=lru_wx,
             lru_bx=lru_bx, lru_lam=lru_lam, conv_b_w=conv_b_w, gla_wa2=gla_wa2, gla_ba2=gla_ba2,
             gla_norm_w=gla_norm_w, att_rel_bias=att_rel_bias, w_out=w_out, ln1_g=ln1_g, ln1_b=ln1_b,
             w_rg=w_rg, b_rg=b_rg, w_re=w_re, b_re=b_re, w_gate=w_gate, w_up=w_up, w_down=w_down,
             ln2_g=ln2_g, ln2_b=ln2_b)
    outs = []
    for bi in range(bsz):
        h = _ln_rows(x[bi], ln0_g, ln0_b)
        for l in range(DEPTH):
            h = _layer(l, h, p)
        outs.append(h.reshape(seq, d))
    return jnp.stack(outs, axis=0)
```

```python
import functools
import math

import jax
import jax.numpy as jnp
import numpy as np
from jax import lax
from jax.experimental import pallas as pl
from jax.experimental.pallas import tpu as pltpu

F32 = jnp.float32
BF16 = jnp.bfloat16

D_MODEL = 2048
DEPTH = 2
CHUNK = 64
GROUP_W = 512
LRU_HEADS = 4
LRU_BLOCK = 128
LRU_CONV = 4
LRU_C = 8.0
SCONV_W = 3
GLA_HEADS = 4
GLA_DK = 64
GLA_DV = 128
GLA_RANK = 16
GLA_TAU = 16.0
ATT_HEADS = 4
ATT_HD = 128
ATT_PREV_CHUNKS = 8
REL_CLIP = 256
N_GROUPS = 4
EXPERTS_PER_GROUP = 4
N_EXPERTS = 16
D_EXPERT = 512
LN_EPS = 1e-5
RMS_EPS = 1e-6
DN_ALPHA = (2 * DEPTH) ** 0.25

LANES = 128
SUBLANES = 8
VMEM_LIMIT = 56 * 1024 * 1024

Z_W = 6144
ZB_AX, ZB_AG, ZB_BB, ZB_BC, ZB_BH, ZB_CQK, ZB_CV, ZB_CG, ZB_DQ, ZB_DK, ZB_DV = range(11)
ZB_CA128 = 44


def _cparams(sem):
    return pltpu.CompilerParams(dimension_semantics=sem, vmem_limit_bytes=VMEM_LIMIT)


def _layer_norm(x, g, b):
    mu = jnp.mean(x, axis=-1, keepdims=True)
    xc = x - mu
    var = jnp.mean(xc * xc, axis=-1, keepdims=True)
    return xc * lax.rsqrt(var + LN_EPS) * g + b


def _sigmoid(x):
    return 1.0 / (1.0 + jnp.exp(-x))


def _softplus(x):
    return jnp.maximum(x, 0.0) + jnp.log(1.0 + jnp.exp(-jnp.abs(x)))


def _silu(x):
    return x * _sigmoid(x)


def _gelu_tanh(x):
    c = math.sqrt(2.0 / math.pi)
    return 0.5 * x * (1.0 + jnp.tanh(c * (x + 0.044715 * (x * x * x))))


ROW_CHUNKS = D_MODEL // LANES


def _rows_chunk(ref, c, n):
    return ref[pl.ds(c, n, stride=ROW_CHUNKS), :]


def _rows_load(ref, n):
    return jnp.concatenate([_rows_chunk(ref, c, n) for c in range(ROW_CHUNKS)], axis=1)


def _rows_store(ref, val):
    n = val.shape[0]
    for c in range(ROW_CHUNKS):
        ref[pl.ds(c, n, stride=ROW_CHUNKS), :] = val[:, c * LANES:(c + 1) * LANES]


def _ln_kernel(x_ref, g_ref, b_ref, o_ref):
    _rows_store(o_ref, _layer_norm(x_ref[...], g_ref[...], b_ref[...]))


def _ln_rows(x, g, b, tm=512):
    t, d = x.shape
    return pl.pallas_call(
        _ln_kernel,
        grid=(t // tm,),
        in_specs=[pl.BlockSpec((tm, d), lambda i: (i, 0)),
                  pl.BlockSpec((1, d), lambda i: (0, 0)),
                  pl.BlockSpec((1, d), lambda i: (0, 0))],
        out_specs=pl.BlockSpec((tm * ROW_CHUNKS, LANES), lambda i: (i, 0)),
        out_shape=jax.ShapeDtypeStruct((t * ROW_CHUNKS, LANES), F32),
        compiler_params=_cparams(("arbitrary",)),
        name="ln0",
    )(x, g.reshape(1, d), b.reshape(1, d))


def _inproj_kernel(h_ref, w_ref, z_ref, hb_ref):
    @pl.when(pl.program_id(1) == 0)
    def _():
        tm = hb_ref.shape[0]
        for c in range(ROW_CHUNKS):
            hb_ref[:, c * LANES:(c + 1) * LANES] = _rows_chunk(h_ref, c, tm).astype(BF16)

    z_ref[...] = jnp.dot(hb_ref[...], w_ref[...], preferred_element_type=F32).astype(z_ref.dtype)


def _inproj(h, t, w_bf, tm=1024, tn=1024):
    d = D_MODEL
    n = w_bf.shape[1]
    tm = min(tm, t)
    return pl.pallas_call(
        _inproj_kernel,
        grid=(t // tm, n // tn),
        in_specs=[pl.BlockSpec((tm * ROW_CHUNKS, LANES), lambda i, j: (i, 0)),
                  pl.BlockSpec((d, tn), lambda i, j: (0, j))],
        out_specs=pl.BlockSpec((tm, tn), lambda i, j: (i, j)),
        out_shape=jax.ShapeDtypeStruct((t, n), BF16),
        scratch_shapes=[pltpu.VMEM((tm, d), BF16)],
        compiler_params=_cparams(("arbitrary", "arbitrary")),
        name="inproj",
    )(h, w_bf)


def _ab_kernel(ax_ref, ag_ref, bb_ref, bc_ref, bh_ref,
               caw_ref, cab_ref, wa_ref, ba_ref, wx_ref, bx_ref, lam_ref, cbw_ref,
               ya_ref, yb_ref, xa_buf, xb_buf, h_st):
    tb = ax_ref.shape[0]
    pad = SUBLANES

    @pl.when(pl.program_id(0) == 0)
    def _():
        xa_buf[0:pad, :] = jnp.zeros((pad, GROUP_W), F32)
        xb_buf[0:pad, :] = jnp.zeros((pad, GROUP_W), F32)
        h_st[...] = jnp.zeros_like(h_st)

    xa_buf[pad:pad + tb, :] = ax_ref[...].astype(F32)
    u = cab_ref[...] + caw_ref[0:1, :] * xa_buf[pad - 3:pad - 3 + tb, :]
    for j in range(1, LRU_CONV):
        off = pad - (LRU_CONV - 1) + j
        u = u + caw_ref[j:j + 1, :] * xa_buf[off:off + tb, :]
    xa_buf[0:pad, :] = xa_buf[tb:tb + pad, :]

    ub = u.astype(BF16)
    r_parts, i_parts = [], []
    for hd in range(LRU_HEADS):
        sl = slice(hd * LRU_BLOCK, (hd + 1) * LRU_BLOCK)
        r_parts.append(jnp.dot(ub[:, sl], wa_ref[hd], preferred_element_type=F32))
        i_parts.append(jnp.dot(ub[:, sl], wx_ref[hd], preferred_element_type=F32))
    r = _sigmoid(jnp.concatenate(r_parts, axis=1) + ba_ref[...])
    ig = _sigmoid(jnp.concatenate(i_parts, axis=1) + bx_ref[...])

    log_a = (-LRU_C * r) * _softplus(-lam_ref[...])
    a = jnp.exp(log_a)
    th = jnp.tanh(log_a)
    mult = jnp.sqrt(-2.0 * th / (1.0 - th))
    bterm = mult * (ig * u)

    rows = lax.broadcasted_iota(jnp.int32, a.shape, 0)
    acum = a
    bcum = bterm
    d = 1
    while d < tb:
        head = rows < d
        b_sh = jnp.where(head, 0.0, pltpu.roll(bcum, d, axis=0))
        a_sh = jnp.where(head, 1.0, pltpu.roll(acum, d, axis=0))
        bcum = acum * b_sh + bcum
        acum = acum * a_sh
        d *= 2
    h = acum * h_st[0:1, :] + bcum
    h_st[0:1, :] = h[tb - 1:tb, :]
    ya_ref[...] = (h * _gelu_tanh(ag_ref[...].astype(F32))).astype(ya_ref.dtype)

    xb_buf[pad:pad + tb, :] = bc_ref[...].astype(F32) * bh_ref[...].astype(F32)
    cv = cbw_ref[0:1, :] * xb_buf[pad - 2:pad - 2 + tb, :]
    for j in range(1, SCONV_W):
        off = pad - (SCONV_W - 1) + j
        cv = cv + cbw_ref[j:j + 1, :] * xb_buf[off:off + tb, :]
    xb_buf[0:pad, :] = xb_buf[tb:tb + pad, :]
    yb_ref[...] = (bb_ref[...].astype(F32) * cv).astype(yb_ref.dtype)


def _mixer_ab(z, caw, cab, wa_bf, ba, wx_bf, bx, lam, cbw, tb=256):
    t = z.shape[0]
    tb = min(tb, t)
    w = GROUP_W

    def zspec(blk):
        return pl.BlockSpec((tb, w), lambda i, blk=blk: (i, blk))

    def full(shape):
        nd = len(shape)
        return pl.BlockSpec(shape, lambda i, nd=nd: (0,) * nd)

    return pl.pallas_call(
        _ab_kernel,
        grid=(t // tb,),
        in_specs=[zspec(ZB_AX), zspec(ZB_AG), zspec(ZB_BB), zspec(ZB_BC), zspec(ZB_BH),
                  full((LRU_CONV, w)), full((1, w)),
                  full((LRU_HEADS, LRU_BLOCK, LRU_BLOCK)), full((1, w)),
                  full((LRU_HEADS, LRU_BLOCK, LRU_BLOCK)), full((1, w)),
                  full((1, w)), full((SCONV_W, w))],
        out_specs=[pl.BlockSpec((tb, w), lambda i: (i, 0)),
                   pl.BlockSpec((tb, w), lambda i: (i, 0))],
        out_shape=[jax.ShapeDtypeStruct((t, w), BF16), jax.ShapeDtypeStruct((t, w), BF16)],
        scratch_shapes=[pltpu.VMEM((tb + SUBLANES, w), F32),
                        pltpu.VMEM((tb + SUBLANES, w), F32),
                        pltpu.VMEM((SUBLANES, w), F32)],
        compiler_params=_cparams(("arbitrary",)),
        name="mixer_ab",
    )(z, z, z, z, z, caw, cab.reshape(1, w), wa_bf, ba.reshape(1, w), wx_bf, bx.reshape(1, w),
      lam.reshape(1, w), cbw)


_NT = (((1,), (1,)), ((), ()))
_TN = (((0,), (0,)), ((), ()))


def _split_bf16(x):
    hi = x.astype(BF16)
    lo = (x - hi.astype(F32)).astype(BF16)
    return hi, lo


def _gla_kernel(qk_ref, v_ref, g_ref, ca_ref, wa2_ref, ba2_ref, nw_ref, y_ref, s_ref):
    tb = qk_ref.shape[0]
    hk = GLA_HEADS * GLA_DK
    hv = GLA_HEADS * GLA_DV
    L = CHUNK

    @pl.when(pl.program_id(0) == 0)
    def _():
        s_ref[...] = jnp.zeros_like(s_ref)

    tri = (lax.broadcasted_iota(jnp.int32, (L, L), 0) >= lax.broadcasted_iota(jnp.int32, (L, L), 1))
    tri_bf = tri.astype(BF16)
    tri4 = jnp.concatenate([tri] * GLA_HEADS, axis=0)
    lane_head = lax.broadcasted_iota(jnp.int32, (L, hk), 1) // GLA_DK
    bd_mask = (lax.broadcasted_iota(jnp.int32, (hk, hv), 0) // GLA_DK
               == lax.broadcasted_iota(jnp.int32, (hk, hv), 1) // GLA_DV)
    ones_bf = jnp.ones((L, LANES), BF16)
    scale = GLA_DK ** -0.5

    for c in range(tb // L):
        rs = slice(c * L, (c + 1) * L)
        q = qk_ref[rs, 0:hk].astype(F32) * scale
        k = qk_ref[rs, hk:2 * hk].astype(F32)
        v = v_ref[rs, :]
        x = jnp.dot(ca_ref[rs, :], wa2_ref[...], preferred_element_type=F32) + ba2_ref[...]
        la = (jnp.minimum(x, 0.0) - jnp.log(1.0 + jnp.exp(-jnp.abs(x)))) * (1.0 / GLA_TAU)
        la_hi, la_lo = _split_bf16(la)
        bcum = (jnp.dot(tri_bf, la_hi, preferred_element_type=F32)
                + jnp.dot(tri_bf, la_lo, preferred_element_type=F32))
        blast = bcum[L - 1:L, :]
        q_dec = q * jnp.exp(bcum)
        k_inv = (k * jnp.exp(-bcum)).astype(BF16)
        k_end = (k * jnp.exp(blast - bcum)).astype(BF16)

        q4 = jnp.concatenate(
            [jnp.where(lane_head == hd, q_dec, 0.0) for hd in range(GLA_HEADS)], axis=0).astype(BF16)
        sc = lax.dot_general(q4, k_inv, _NT, preferred_element_type=F32)
        sc = jnp.where(tri4, sc, 0.0).astype(BF16)
        oi = jnp.dot(sc, v, preferred_element_type=F32)
        o_intra = jnp.concatenate(
            [oi[hd * L:(hd + 1) * L, hd * GLA_DV:(hd + 1) * GLA_DV] for hd in range(GLA_HEADS)], axis=1)

        s = s_ref[...]
        o = o_intra + jnp.dot(q_dec.astype(BF16), s.astype(BF16), preferred_element_type=F32)

        kv = lax.dot_general(k_end, v, _TN, preferred_element_type=F32)
        bl_col = (lax.dot_general(la_hi, ones_bf, _TN, preferred_element_type=F32)
                  + lax.dot_general(la_lo, ones_bf, _TN, preferred_element_type=F32))
        dcol = jnp.exp(bl_col)
        s_ref[...] = s * jnp.concatenate([dcol] * (hv // LANES), axis=1) + jnp.where(bd_mask, kv, 0.0)

        parts = []
        for hd in range(GLA_HEADS):
            oh = o[:, hd * GLA_DV:(hd + 1) * GLA_DV]
            parts.append(oh * lax.rsqrt(jnp.mean(oh * oh, axis=-1, keepdims=True) + RMS_EPS))
        on = jnp.concatenate(parts, axis=1) * nw_ref[...]
        y_ref[rs, :] = (on * _silu(g_ref[rs, :].astype(F32))).astype(y_ref.dtype)


def _mixer_gla(z, wa2_pad_bf, ba2, nw, tb=256):
    t = z.shape[0]
    tb = min(tb, t)
    w = GROUP_W
    hk = GLA_HEADS * GLA_DK
    return pl.pallas_call(
        _gla_kernel,
        grid=(t // tb,),
        in_specs=[pl.BlockSpec((tb, w), lambda i: (i, ZB_CQK)),
                  pl.BlockSpec((tb, w), lambda i: (i, ZB_CV)),
                  pl.BlockSpec((tb, w), lambda i: (i, ZB_CG)),
                  pl.BlockSpec((tb, LANES), lambda i: (i, ZB_CA128)),
                  pl.BlockSpec((LANES, hk), lambda i: (0, 0)),
                  pl.BlockSpec((1, hk), lambda i: (0, 0)),
                  pl.BlockSpec((1, w), lambda i: (0, 0))],
        out_specs=pl.BlockSpec((tb, w), lambda i: (i, 0)),
        out_shape=jax.ShapeDtypeStruct((t, w), BF16),
        scratch_shapes=[pltpu.VMEM((hk, w), F32)],
        compiler_params=_cparams(("arbitrary",)),
        name="mixer_gla",
    )(z, z, z, z, wa2_pad_bf, ba2.reshape(1, hk), nw.reshape(1, w))


ATT_TQ = 512
ATT_SUB = 128
ATT_KW = ATT_SUB + ATT_PREV_CHUNKS * CHUNK


def _att_kernel(q_ref, kp_ref, kc_ref, vp_ref, vc_ref, bias_ref, y_ref, kcat, vcat):
    i = pl.program_id(0)
    kcat[0:ATT_TQ, :] = kp_ref[...]
    kcat[ATT_TQ:2 * ATT_TQ, :] = kc_ref[...]
    vcat[0:ATT_TQ, :] = vp_ref[...]
    vcat[ATT_TQ:2 * ATT_TQ, :] = vc_ref[...]
    col = lax.broadcasted_iota(jnp.int32, (ATT_SUB, ATT_KW), 1)
    scale = ATT_HD ** -0.5
    for j in range(ATT_TQ // ATT_SUB):
        r0 = j * ATT_SUB
        n_invalid = jnp.where(i == 0, ATT_TQ - r0, 0)
        dead = col < n_invalid
        outs = []
        for hd in range(ATT_HEADS):
            ls = slice(hd * ATT_HD, (hd + 1) * ATT_HD)
            qh = q_ref[r0:r0 + ATT_SUB, ls]
            kh = kcat[r0:r0 + ATT_KW, ls]
            vh = vcat[r0:r0 + ATT_KW, ls]
            s = lax.dot_general(qh, kh, _NT, preferred_element_type=F32) * scale + bias_ref[hd]
            s = jnp.where(dead, -jnp.inf, s)
            m = jnp.max(s, axis=-1, keepdims=True)
            p = jnp.exp(s - m)
            l = jnp.sum(p, axis=-1, keepdims=True)
            o = jnp.dot(p.astype(BF16), vh, preferred_element_type=F32)
            outs.append(o / l)
        y_ref[r0:r0 + ATT_SUB, :] = jnp.concatenate(outs, axis=1).astype(y_ref.dtype)


def _att_bias_table(rel_bias):
    nh = rel_bias.shape[0]
    span = ATT_PREV_CHUNKS * CHUNK
    period = ATT_KW + ATT_SUB
    edge = rel_bias[:, 2 * REL_CLIP:2 * REL_CLIP + 1]
    n_flat = span - REL_CLIP + 1
    ramp = rel_bias[:, 2 * REL_CLIP - (ATT_KW - n_flat):2 * REL_CLIP][:, ::-1]
    vec = jnp.concatenate([jnp.broadcast_to(edge, (nh, n_flat)), ramp,
                           jnp.broadcast_to(edge, (nh, period - ATT_KW))], axis=1)
    tab = jnp.tile(vec, (1, ATT_SUB))[:, :ATT_SUB * (period - 1)].reshape(nh, ATT_SUB, period - 1)[:, :, :ATT_KW]
    r = np.arange(ATT_SUB)[:, None]
    c = np.arange(ATT_KW)[None, :]
    rel = c - CHUNK * (r // CHUNK)
    valid = (rel >= 0) & (rel < (ATT_PREV_CHUNKS + 1) * CHUNK)
    return jnp.where(jnp.asarray(valid)[None], tab.astype(F32), -jnp.inf)


def _mixer_att(z, bias_tab):
    t = z.shape[0]
    w = GROUP_W
    tq = ATT_TQ

    def prev(i):
        return jnp.maximum(i - 1, 0)

    return pl.pallas_call(
        _att_kernel,
        grid=(t // tq,),
        in_specs=[pl.BlockSpec((tq, w), lambda i: (i, ZB_DQ)),
                  pl.BlockSpec((tq, w), lambda i: (prev(i), ZB_DK)),
                  pl.BlockSpec((tq, w), lambda i: (i, ZB_DK)),
                  pl.BlockSpec((tq, w), lambda i: (prev(i), ZB_DV)),
                  pl.BlockSpec((tq, w), lambda i: (i, ZB_DV)),
                  pl.BlockSpec((ATT_HEADS, ATT_SUB, ATT_KW), lambda i: (0, 0, 0))],
        out_specs=pl.BlockSpec((tq, w), lambda i: (i, 0)),
        out_shape=jax.ShapeDtypeStruct((t, w), BF16),
        scratch_shapes=[pltpu.VMEM((2 * tq, w), BF16), pltpu.VMEM((2 * tq, w), BF16)],
        compiler_params=_cparams(("arbitrary",)),
        name="mixer_att",
    )(z, z, z, z, z, bias_tab)


R_ROWS = 32
PAIRS = ((0, 1), (0, 2), (0, 3), (1, 3), (1, 2), (2, 3))
SLOT0 = (0, 0, 0, 1, 1, 3)
SLOT1 = (1, 2, 3, 3, 2, 2)
N_PAIRS = len(PAIRS)
N_CLASSES = N_GROUPS * N_PAIRS
C_ROWS = 32


def _outproj_kernel(ya_ref, yb_ref, yc_ref, yd_ref, w_ref, h_ref, g_ref, b_ref, wr_ref, br_ref,
                    h1_ref, meta_ref, counts_ref, cnt_ref):
    w = GROUP_W
    acc = jnp.dot(ya_ref[...], w_ref[0:w, :], preferred_element_type=F32)
    acc = acc + jnp.dot(yb_ref[...], w_ref[w:2 * w, :], preferred_element_type=F32)
    acc = acc + jnp.dot(yc_ref[...], w_ref[2 * w:3 * w, :], preferred_element_type=F32)
    acc = acc + jnp.dot(yd_ref[...], w_ref[3 * w:4 * w, :], preferred_element_type=F32)
    tm = ya_ref.shape[0]
    h1 = _layer_norm(DN_ALPHA * _rows_load(h_ref, tm) + acc, g_ref[...], b_ref[...])
    _rows_store(h1_ref, h1)

    h1_hi, h1_lo = _split_bf16(h1)
    lt = (lax.dot_general(wr_ref[0], h1_hi, _NT, preferred_element_type=F32)
          + lax.dot_general(wr_ref[0], h1_lo, _NT, preferred_element_type=F32)
          + lax.dot_general(wr_ref[1], h1_hi, _NT, preferred_element_type=F32)) + br_ref[:, 0:1]
    ng = N_GROUPS
    epg = EXPERTS_PER_GROUP
    lg = lt[0:ng, :]
    gm = jnp.max(lg, axis=0, keepdims=True)
    g_val = 1.0 / jnp.sum(jnp.exp(lg - gm), axis=0, keepdims=True)
    gi = jnp.full(gm.shape, ng - 1, jnp.int32)
    for g in range(ng - 2, -1, -1):
        gi = jnp.where(lg[g:g + 1, :] == gm, g, gi)
    e = []
    for j in range(epg):
        ej = lt[ng + (ng - 1) * epg + j:ng + (ng - 1) * epg + j + 1, :]
        for g in range(ng - 2, -1, -1):
            ej = jnp.where(gi == g, lt[ng + g * epg + j:ng + g * epg + j + 1, :], ej)
        e.append(ej)
    v1 = jnp.maximum(jnp.maximum(e[0], e[1]), jnp.maximum(e[2], e[3]))
    i1 = jnp.full(v1.shape, epg - 1, jnp.int32)
    for j in range(epg - 2, -1, -1):
        i1 = jnp.where(e[j] == v1, j, i1)
    neg = -jnp.inf
    e2 = [jnp.where(i1 == j, neg, e[j]) for j in range(epg)]
    v2 = jnp.maximum(jnp.maximum(e2[0], e2[1]), jnp.maximum(e2[2], e2[3]))
    i2 = jnp.full(v1.shape, epg - 1, jnp.int32)
    for j in range(epg - 2, -1, -1):
        i2 = jnp.where(e2[j] == v2, j, i2)
    t2 = jnp.exp(v2 - v1)
    w1 = g_val / (1.0 + t2)
    w2 = g_val * t2 / (1.0 + t2)
    lo_e = jnp.minimum(i1, i2)
    hi_e = jnp.maximum(i1, i2)
    pkey = lo_e * epg + hi_e
    pair = jnp.zeros_like(pkey)
    for pi, (pa, pb) in enumerate(PAIRS):
        pair = jnp.where(pkey == pa * epg + pb, pi, pair)
    cls = gi * N_PAIRS + pair
    wloc = [jnp.where(i1 == j, w1, 0.0) + jnp.where(i2 == j, w2, 0.0) for j in range(epg)]
    ws0 = jnp.zeros_like(w1)
    ws1 = jnp.zeros_like(w1)
    for pi in range(N_PAIRS):
        ws0 = jnp.where(pair == pi, wloc[SLOT0[pi]], ws0)
        ws1 = jnp.where(pair == pi, wloc[SLOT1[pi]], ws1)

    @pl.when(pl.program_id(0) == 0)
    def _():
        cnt_ref[...] = jnp.zeros_like(cnt_ref)

    tm = lt.shape[1]
    crow = lax.broadcasted_iota(jnp.int32, (C_ROWS, tm), 0)
    ohf = jnp.where(crow == cls, 1.0, 0.0)
    upper = (lax.broadcasted_iota(jnp.int32, (tm, tm), 0)
             < lax.broadcasted_iota(jnp.int32, (tm, tm), 1)).astype(BF16)
    before = jnp.dot(ohf.astype(BF16), upper, preferred_element_type=F32)
    carry = cnt_ref[...]
    rank = jnp.sum(ohf * (before + carry[:, 0:1]), axis=0, keepdims=True)
    cnt_new = carry + jnp.sum(ohf, axis=1, keepdims=True)
    cnt_ref[...] = cnt_new
    counts_ref[...] = cnt_new
    meta_ref[...] = jnp.concatenate(
        [cls.astype(F32), ws0, ws1, rank, jnp.zeros((SUBLANES - 4, tm), F32)], axis=0)


def _outproj_router(ya, yb, yc, yd, w_out_bf, h, t, g, b, wr_t_bf, br_col, tm=512):
    d = D_MODEL
    w = GROUP_W
    tm = min(tm, t)

    def ys():
        return pl.BlockSpec((tm, w), lambda i: (i, 0))

    def rows():
        return pl.BlockSpec((tm * ROW_CHUNKS, LANES), lambda i: (i, 0))

    return pl.pallas_call(
        _outproj_kernel,
        grid=(t // tm,),
        in_specs=[ys(), ys(), ys(), ys(),
                  pl.BlockSpec((d, d), lambda i: (0, 0)),
                  rows(),
                  pl.BlockSpec((1, d), lambda i: (0, 0)),
                  pl.BlockSpec((1, d), lambda i: (0, 0)),
                  pl.BlockSpec((2, R_ROWS, d), lambda i: (0, 0, 0)),
                  pl.BlockSpec((R_ROWS, LANES), lambda i: (0, 0))],
        out_specs=[rows(),
                   pl.BlockSpec((SUBLANES, tm), lambda i: (0, i)),
                   pl.BlockSpec((C_ROWS, LANES), lambda i: (0, 0))],
        out_shape=[jax.ShapeDtypeStruct((t * ROW_CHUNKS, LANES), F32),
                   jax.ShapeDtypeStruct((SUBLANES, t), F32),
                   jax.ShapeDtypeStruct((C_ROWS, LANES), F32)],
        scratch_shapes=[pltpu.VMEM((C_ROWS, LANES), F32)],
        compiler_params=_cparams(("arbitrary",)),
        name="outproj_router",
    )(ya, yb, yc, yd, w_out_bf, h, g.reshape(1, d), b.reshape(1, d), wr_t_bf, br_col)


MOE_TM = 256
DUMP_TOKENS = 2 * MOE_TM


def _moe_kernel(te0_ref, te1_ref, nt_ref,
                gtok_ref, gtok_next_ref, stok_ref, w_ref, wg0_ref, wu0_ref, wd0_ref, wg1_ref, wu1_ref, wd1_ref,
                g_ref, b_ref, hin_ref, hout_ref, xg, yo, gsem, ssem, *, n_tok):
    k = pl.program_id(0)
    nt = nt_ref[0]
    tm = MOE_TM
    rc = ROW_CHUNKS
    slot = lax.rem(k, 2)
    other = 1 - slot

    def gather_row(idx_ref, s, r):
        src = hin_ref.at[pl.ds(pl.multiple_of(idx_ref[0, 0, r] * rc, rc), rc)]
        return pltpu.make_async_copy(src, xg.at[s, pl.ds(r * rc, rc)], gsem.at[s])

    def scatter_row(s, r):
        dst = hout_ref.at[pl.ds(pl.multiple_of(stok_ref[0, 0, r] * rc, rc), rc)]
        return pltpu.make_async_copy(yo.at[s, pl.ds(r * rc, rc)], dst, ssem.at[s])

    def wait_gather(s):
        pltpu.make_async_copy(hin_ref.at[pl.ds(0, tm * rc)], xg.at[s], gsem.at[s]).wait()

    def wait_scatter(s):
        pltpu.make_async_copy(yo.at[s], hout_ref.at[pl.ds(0, tm * rc)], ssem.at[s]).wait()

    @pl.when(k == 0)
    def _():
        yo[...] = jnp.zeros_like(yo)
        for s in range(2):
            pltpu.make_async_copy(yo.at[s], hout_ref.at[pl.ds((n_tok + s * tm) * rc, tm * rc)], ssem.at[s]).start()
        for s in range(2):
            wait_scatter(s)
        for r in range(tm):
            gather_row(gtok_ref, 0, r).start()

    @pl.when(k < nt)
    def _():
        wait_gather(slot)

        @pl.when(k >= 2)
        def _():
            wait_scatter(slot)

        for r in range(tm):
            gather_row(gtok_next_ref, other, r).start()

        x = _rows_load(xg.at[slot], tm)
        xb = x.astype(BF16)
        h0 = _silu(jnp.dot(xb, wg0_ref[0], preferred_element_type=F32)) * jnp.dot(
            xb, wu0_ref[0], preferred_element_type=F32)
        h1 = _silu(jnp.dot(xb, wg1_ref[0], preferred_element_type=F32)) * jnp.dot(
            xb, wu1_ref[0], preferred_element_type=F32)
        ffn = jnp.dot((h0 * w_ref[:, 1:2]).astype(BF16), wd0_ref[0], preferred_element_type=F32)
        ffn = ffn + jnp.dot((h1 * w_ref[:, 2:3]).astype(BF16), wd1_ref[0], preferred_element_type=F32)
        _rows_store(yo.at[slot], _layer_norm(DN_ALPHA * x + ffn, g_ref[...], b_ref[...]))
        for r in range(tm):
            scatter_row(slot, r).start()

        @pl.when(k == nt - 1)
        def _():
            wait_gather(other)

            @pl.when(k >= 1)
            def _():
                wait_scatter(other)
            wait_scatter(slot)


def _moe_sorted(h1, n_tok, gtok, stok, wsort, te0, te1, nt, wg_bf, wu_bf, wd_bf, g, b):
    d = D_MODEL
    tm = MOE_TM
    de = wg_bf.shape[2]
    nt_max = gtok.shape[0] // tm

    def wspec(shape, slot):
        if slot == 0:
            return pl.BlockSpec(shape, lambda k, te0, te1, nt: (te0[k], 0, 0))
        return pl.BlockSpec(shape, lambda k, te0, te1, nt: (te1[k], 0, 0))

    def cur(k, te0, te1, nt):
        return jnp.minimum(k, nt[0] - 1)

    def nxt(k, te0, te1, nt):
        return jnp.minimum(k + 1, nt[0] - 1)

    def idx_spec(which):
        return pl.BlockSpec((1, 1, tm), lambda *a: (which(*a), 0, 0), memory_space=pltpu.SMEM)

    return pl.pallas_call(
        functools.partial(_moe_kernel, n_tok=n_tok),
        grid_spec=pltpu.PrefetchScalarGridSpec(
            num_scalar_prefetch=3,
            grid=(nt_max,),
            in_specs=[idx_spec(cur), idx_spec(nxt), idx_spec(cur),
                      pl.BlockSpec((tm, wsort.shape[1]), lambda *a: (cur(*a), 0)),
                      wspec((1, d, de), 0), wspec((1, d, de), 0), wspec((1, de, d), 0),
                      wspec((1, d, de), 1), wspec((1, d, de), 1), wspec((1, de, d), 1),
                      pl.BlockSpec((1, d), lambda k, *_: (0, 0)),
                      pl.BlockSpec((1, d), lambda k, *_: (0, 0)),
                      pl.BlockSpec(memory_space=pl.ANY)],
            out_specs=pl.BlockSpec(memory_space=pl.ANY),
            scratch_shapes=[pltpu.VMEM((2, tm * ROW_CHUNKS, LANES), F32),
                            pltpu.VMEM((2, tm * ROW_CHUNKS, LANES), F32),
                            pltpu.SemaphoreType.DMA((2,)),
                            pltpu.SemaphoreType.DMA((2,))]),
        out_shape=jax.ShapeDtypeStruct(((n_tok + DUMP_TOKENS) * ROW_CHUNKS, LANES), F32),
        compiler_params=_cparams(("arbitrary",)),
        name="moe_sorted",
    )(te0, te1, nt, gtok.reshape(nt_max, 1, tm), gtok.reshape(nt_max, 1, tm), stok.reshape(nt_max, 1, tm), wsort,
      wg_bf, wu_bf, wd_bf, wg_bf, wu_bf, wd_bf, g.reshape(1, d), b.reshape(1, d), h1)


def _moe_plan(meta, counts, t):
    tm = MOE_TM
    nt_max = t // tm + N_CLASSES
    n_pad = nt_max * tm
    cls = meta[0].astype(jnp.int32)
    rank = meta[3].astype(jnp.int32)
    cnt = counts[:N_CLASSES, 0].astype(jnp.int32)
    tiles = (cnt + tm - 1) // tm
    tile_end = jnp.cumsum(tiles)
    tile_start = tile_end - tiles
    n_tiles = tile_end[-1]
    onehot = (cls[:, None] == jnp.arange(N_CLASSES)[None, :]).astype(F32)
    pos = jnp.dot(onehot, tile_start.astype(F32)).astype(jnp.int32) * tm + rank
    k = jnp.arange(nt_max)
    ksrc = jnp.minimum(k, jnp.maximum(n_tiles - 1, 0))
    tcls_oh = ((jnp.sum(ksrc[:, None] >= tile_end[None, :], axis=1))[:, None]
               == jnp.arange(N_CLASSES)[None, :]).astype(jnp.int32)
    tcls = jnp.sum(tcls_oh * jnp.arange(N_CLASSES)[None, :], axis=1)
    grp = tcls // N_PAIRS
    pr_oh = ((tcls % N_PAIRS)[:, None] == jnp.arange(N_PAIRS)[None, :]).astype(jnp.int32)
    te0 = grp * EXPERTS_PER_GROUP + jnp.sum(pr_oh * jnp.asarray(SLOT0, jnp.int32)[None, :], axis=1)
    te1 = grp * EXPERTS_PER_GROUP + jnp.sum(pr_oh * jnp.asarray(SLOT1, jnp.int32)[None, :], axis=1)
    payload = jnp.stack([jnp.arange(t, dtype=F32), meta[1], meta[2], jnp.ones((t,), F32)], axis=1)
    wsort = jnp.zeros((n_pad, 4), F32).at[pos].set(payload)
    gtok = wsort[:, 0].astype(jnp.int32)
    row = jnp.arange(n_pad, dtype=jnp.int32)
    dump = t + lax.rem(row // tm, 2) * tm + lax.rem(row, tm)
    stok = jnp.where(wsort[:, 3] > 0.5, gtok, dump)
    return gtok, stok, wsort, te0.astype(jnp.int32), te1.astype(jnp.int32), n_tiles.astype(jnp.int32).reshape(1)


def _prep_w_in(w_in_l):
    d = w_in_l.shape[0]
    ca0 = 4096
    ca1 = ca0 + GLA_RANK
    parts = [w_in_l[:, :ca0], w_in_l[:, ca1:], w_in_l[:, ca0:ca1]]
    used = ca0 + (w_in_l.shape[1] - ca1) + GLA_RANK
    parts.append(jnp.zeros((d, Z_W - used), w_in_l.dtype))
    return jnp.concatenate(parts, axis=1).astype(BF16)


def _prep_router(w_rg_l, b_rg_l, w_re_l, b_re_l):
    d = w_rg_l.shape[0]
    wr = jnp.concatenate([w_rg_l, w_re_l], axis=1)
    wr_t = jnp.zeros((R_ROWS, d), F32).at[:wr.shape[1]].set(wr.T)
    wr_hi = wr_t.astype(BF16)
    wr_lo = (wr_t - wr_hi.astype(F32)).astype(BF16)
    wr_t = jnp.stack([wr_hi, wr_lo], axis=0)
    br = jnp.concatenate([b_rg_l, b_re_l])
    br_col = jnp.zeros((R_ROWS, LANES), F32).at[:br.shape[0], :].set(br[:, None])
    return wr_t, br_col


def _layer(l, h, t, p):
    z = _inproj(h, t, _prep_w_in(p["w_in"][l]))
    ya, yb = _mixer_ab(z, p["conv_a_w"][l], p["conv_a_b"][l], p["lru_wa"][l].astype(BF16), p["lru_ba"][l],
                       p["lru_wx"][l].astype(BF16), p["lru_bx"][l], p["lru_lam"][l], p["conv_b_w"][l])
    hk = GLA_HEADS * GLA_DK
    wa2_pad = jnp.zeros((LANES, hk), F32).at[:GLA_RANK].set(p["gla_wa2"][l]).astype(BF16)
    yc = _mixer_gla(z, wa2_pad, p["gla_ba2"][l], p["gla_norm_w"][l])
    yd = _mixer_att(z, _att_bias_table(p["att_rel_bias"][l]))
    wr_t, br_col = _prep_router(p["w_rg"][l], p["b_rg"][l], p["w_re"][l], p["b_re"][l])
    h1, meta, counts = _outproj_router(ya, yb, yc, yd, p["w_out"][l].astype(BF16), h, t, p["ln1_g"][l],
                                       p["ln1_b"][l], wr_t, br_col)
    gtok, stok, wsort, te0, te1, nt = _moe_plan(meta, counts, t)
    return _moe_sorted(h1, t, gtok, stok, wsort, te0, te1, nt, p["w_gate"][l].astype(BF16),
                       p["w_up"][l].astype(BF16), p["w_down"][l].astype(BF16), p["ln2_g"][l], p["ln2_b"][l])


def kernel(x, ln0_g, ln0_b, w_in, conv_a_w, conv_a_b, lru_wa, lru_ba, lru_wx, lru_bx, lru_lam, conv_b_w,
           gla_wa2, gla_ba2, gla_norm_w, att_rel_bias, w_out, ln1_g, ln1_b, w_rg, b_rg, w_re, b_re,
           w_gate, w_up, w_down, ln2_g, ln2_b):
    bsz, seq, d = x.shape
    p = dict(w_in=w_in, conv_a_w=conv_a_w, conv_a_b=conv_a_b, lru_wa=lru_wa, lru_ba=lru_ba, lru_wx=lru_wx,
             lru_bx=lru_bx, lru_lam=lru_lam, conv_b_w=conv_b_w, gla_wa2=gla_wa2, gla_ba2=gla_ba2,
             gla_norm_w=gla_norm_w, att_rel_bias=att_rel_bias, w_out=w_out, ln1_g=ln1_g, ln1_b=ln1_b,
             w_rg=w_rg, b_rg=b_rg, w_re=w_re, b_re=b_re, w_gate=w_gate, w_up=w_up, w_down=w_down,
             ln2_g=ln2_g, ln2_b=ln2_b)
    outs = []
    for bi in range(bsz):
        h = _ln_rows(x[bi], ln0_g, ln0_b)
        for l in range(DEPTH):
            h = _layer(l, h, seq, p)
        outs.append(h[:seq * ROW_CHUNKS].reshape(seq, d))
    return jnp.stack(outs, axis=0)
```

```python
import functools
import math

import jax
import jax.numpy as jnp
import numpy as np
from jax import lax
from jax.experimental import pallas as pl
from jax.experimental.pallas import tpu as pltpu

F32 = jnp.float32
BF16 = jnp.bfloat16

D_MODEL = 2048
DEPTH = 2
CHUNK = 64
GROUP_W = 512
LRU_HEADS = 4
LRU_BLOCK = 128
LRU_CONV = 4
LRU_C = 8.0
SCONV_W = 3
GLA_HEADS = 4
GLA_DK = 64
GLA_DV = 128
GLA_RANK = 16
GLA_TAU = 16.0
ATT_HEADS = 4
ATT_HD = 128
ATT_PREV_CHUNKS = 8
REL_CLIP = 256
N_GROUPS = 4
EXPERTS_PER_GROUP = 4
N_EXPERTS = 16
D_EXPERT = 512
LN_EPS = 1e-5
RMS_EPS = 1e-6
DN_ALPHA = (2 * DEPTH) ** 0.25

LANES = 128
SUBLANES = 8
VMEM_LIMIT = 56 * 1024 * 1024

Z_W = 6144
ZB_AX, ZB_AG, ZB_BB, ZB_BC, ZB_BH, ZB_CQK, ZB_CV, ZB_CG, ZB_DQ, ZB_DK, ZB_DV = range(11)
ZB_CA128 = 44


def _cparams(sem):
    return pltpu.CompilerParams(dimension_semantics=sem, vmem_limit_bytes=VMEM_LIMIT)


def _layer_norm(x, g, b):
    mu = jnp.mean(x, axis=-1, keepdims=True)
    xc = x - mu
    var = jnp.mean(xc * xc, axis=-1, keepdims=True)
    return xc * lax.rsqrt(var + LN_EPS) * g + b


def _sigmoid(x):
    return 1.0 / (1.0 + jnp.exp(-x))


def _softplus(x):
    return jnp.maximum(x, 0.0) + jnp.log(1.0 + jnp.exp(-jnp.abs(x)))


def _silu(x):
    return x * _sigmoid(x)


def _gelu_tanh(x):
    c = math.sqrt(2.0 / math.pi)
    return 0.5 * x * (1.0 + jnp.tanh(c * (x + 0.044715 * (x * x * x))))


ROW_CHUNKS = D_MODEL // LANES


def _rows_chunk(ref, c, n):
    return ref[pl.ds(c, n, stride=ROW_CHUNKS), :]


def _rows_load(ref, n):
    return jnp.concatenate([_rows_chunk(ref, c, n) for c in range(ROW_CHUNKS)], axis=1)


def _rows_store(ref, val):
    n = val.shape[0]
    for c in range(ROW_CHUNKS):
        ref[pl.ds(c, n, stride=ROW_CHUNKS), :] = val[:, c * LANES:(c + 1) * LANES]


def _ln_kernel(x_ref, g_ref, b_ref, o_ref):
    _rows_store(o_ref, _layer_norm(x_ref[...], g_ref[...], b_ref[...]))


def _ln_rows(x, g, b, tm=512):
    t, d = x.shape
    return pl.pallas_call(
        _ln_kernel,
        grid=(t // tm,),
        in_specs=[pl.BlockSpec((tm, d), lambda i: (i, 0)),
                  pl.BlockSpec((1, d), lambda i: (0, 0)),
                  pl.BlockSpec((1, d), lambda i: (0, 0))],
        out_specs=pl.BlockSpec((tm * ROW_CHUNKS, LANES), lambda i: (i, 0)),
        out_shape=jax.ShapeDtypeStruct((t * ROW_CHUNKS, LANES), F32),
        compiler_params=_cparams(("arbitrary",)),
        name="ln0",
    )(x, g.reshape(1, d), b.reshape(1, d))


def _stream_to_2d_kernel(h_ref, o_ref):
    tm = o_ref.shape[0]
    for c in range(ROW_CHUNKS):
        o_ref[:, c * LANES:(c + 1) * LANES] = _rows_chunk(h_ref, c, tm)


def _stream_to_2d(h, t, tm=512):
    return pl.pallas_call(
        _stream_to_2d_kernel,
        grid=(t // tm,),
        in_specs=[pl.BlockSpec((tm * ROW_CHUNKS, LANES), lambda i: (i, 0))],
        out_specs=pl.BlockSpec((tm, D_MODEL), lambda i: (i, 0)),
        out_shape=jax.ShapeDtypeStruct((t, D_MODEL), F32),
        compiler_params=_cparams(("arbitrary",)),
        name="stream_to_2d",
    )(h)


def _inproj_kernel(h_ref, w_ref, z_ref, hb_ref):
    @pl.when(pl.program_id(1) == 0)
    def _():
        tm = hb_ref.shape[0]
        for c in range(ROW_CHUNKS):
            hb_ref[:, c * LANES:(c + 1) * LANES] = _rows_chunk(h_ref, c, tm).astype(BF16)

    z_ref[...] = jnp.dot(hb_ref[...], w_ref[...], preferred_element_type=F32).astype(z_ref.dtype)


def _inproj(h, t, w_bf, tm=1024, tn=1024):
    d = D_MODEL
    n = w_bf.shape[1]
    tm = min(tm, t)
    return pl.pallas_call(
        _inproj_kernel,
        grid=(t // tm, n // tn),
        in_specs=[pl.BlockSpec((tm * ROW_CHUNKS, LANES), lambda i, j: (i, 0)),
                  pl.BlockSpec((d, tn), lambda i, j: (0, j))],
        out_specs=pl.BlockSpec((tm, tn), lambda i, j: (i, j)),
        out_shape=jax.ShapeDtypeStruct((t, n), BF16),
        scratch_shapes=[pltpu.VMEM((tm, d), BF16)],
        compiler_params=_cparams(("arbitrary", "arbitrary")),
        name="inproj",
    )(h, w_bf)


def _ab_kernel(ax_ref, ag_ref, bb_ref, bc_ref, bh_ref,
               caw_ref, cab_ref, wa_ref, ba_ref, wx_ref, bx_ref, lam_ref, cbw_ref,
               ya_ref, yb_ref, xa_buf, xb_buf, h_st):
    tb = ax_ref.shape[0]
    pad = SUBLANES

    @pl.when(pl.program_id(0) == 0)
    def _():
        xa_buf[0:pad, :] = jnp.zeros((pad, GROUP_W), F32)
        xb_buf[0:pad, :] = jnp.zeros((pad, GROUP_W), F32)
        h_st[...] = jnp.zeros_like(h_st)

    xa_buf[pad:pad + tb, :] = ax_ref[...].astype(F32)
    u = cab_ref[...] + caw_ref[0:1, :] * xa_buf[pad - 3:pad - 3 + tb, :]
    for j in range(1, LRU_CONV):
        off = pad - (LRU_CONV - 1) + j
        u = u + caw_ref[j:j + 1, :] * xa_buf[off:off + tb, :]
    xa_buf[0:pad, :] = xa_buf[tb:tb + pad, :]

    ub = u.astype(BF16)
    r_parts, i_parts = [], []
    for hd in range(LRU_HEADS):
        sl = slice(hd * LRU_BLOCK, (hd + 1) * LRU_BLOCK)
        r_parts.append(jnp.dot(ub[:, sl], wa_ref[hd], preferred_element_type=F32))
        i_parts.append(jnp.dot(ub[:, sl], wx_ref[hd], preferred_element_type=F32))
    r = _sigmoid(jnp.concatenate(r_parts, axis=1) + ba_ref[...])
    ig = _sigmoid(jnp.concatenate(i_parts, axis=1) + bx_ref[...])

    log_a = (-LRU_C * r) * _softplus(-lam_ref[...])
    a = jnp.exp(log_a)
    th = jnp.tanh(log_a)
    mult = jnp.sqrt(-2.0 * th / (1.0 - th))
    bterm = mult * (ig * u)

    rows = lax.broadcasted_iota(jnp.int32, a.shape, 0)
    acum = a
    bcum = bterm
    d = 1
    while d < tb:
        head = rows < d
        b_sh = jnp.where(head, 0.0, pltpu.roll(bcum, d, axis=0))
        a_sh = jnp.where(head, 1.0, pltpu.roll(acum, d, axis=0))
        bcum = acum * b_sh + bcum
        acum = acum * a_sh
        d *= 2
    h = acum * h_st[0:1, :] + bcum
    h_st[0:1, :] = h[tb - 1:tb, :]
    ya_ref[...] = (h * _gelu_tanh(ag_ref[...].astype(F32))).astype(ya_ref.dtype)

    xb_buf[pad:pad + tb, :] = bc_ref[...].astype(F32) * bh_ref[...].astype(F32)
    cv = cbw_ref[0:1, :] * xb_buf[pad - 2:pad - 2 + tb, :]
    for j in range(1, SCONV_W):
        off = pad - (SCONV_W - 1) + j
        cv = cv + cbw_ref[j:j + 1, :] * xb_buf[off:off + tb, :]
    xb_buf[0:pad, :] = xb_buf[tb:tb + pad, :]
    yb_ref[...] = (bb_ref[...].astype(F32) * cv).astype(yb_ref.dtype)


def _mixer_ab(z, caw, cab, wa_bf, ba, wx_bf, bx, lam, cbw, tb=256):
    t = z.shape[0]
    tb = min(tb, t)
    w = GROUP_W

    def zspec(blk):
        return pl.BlockSpec((tb, w), lambda i, blk=blk: (i, blk))

    def full(shape):
        nd = len(shape)
        return pl.BlockSpec(shape, lambda i, nd=nd: (0,) * nd)

    return pl.pallas_call(
        _ab_kernel,
        grid=(t // tb,),
        in_specs=[zspec(ZB_AX), zspec(ZB_AG), zspec(ZB_BB), zspec(ZB_BC), zspec(ZB_BH),
                  full((LRU_CONV, w)), full((1, w)),
                  full((LRU_HEADS, LRU_BLOCK, LRU_BLOCK)), full((1, w)),
                  full((LRU_HEADS, LRU_BLOCK, LRU_BLOCK)), full((1, w)),
                  full((1, w)), full((SCONV_W, w))],
        out_specs=[pl.BlockSpec((tb, w), lambda i: (i, 0)),
                   pl.BlockSpec((tb, w), lambda i: (i, 0))],
        out_shape=[jax.ShapeDtypeStruct((t, w), BF16), jax.ShapeDtypeStruct((t, w), BF16)],
        scratch_shapes=[pltpu.VMEM((tb + SUBLANES, w), F32),
                        pltpu.VMEM((tb + SUBLANES, w), F32),
                        pltpu.VMEM((SUBLANES, w), F32)],
        compiler_params=_cparams(("arbitrary",)),
        name="mixer_ab",
    )(z, z, z, z, z, caw, cab.reshape(1, w), wa_bf, ba.reshape(1, w), wx_bf, bx.reshape(1, w),
      lam.reshape(1, w), cbw)


_NT = (((1,), (1,)), ((), ()))
_TN = (((0,), (0,)), ((), ()))


def _split_bf16(x):
    hi = x.astype(BF16)
    lo = (x - hi.astype(F32)).astype(BF16)
    return hi, lo


def _gla_kernel(qk_ref, v_ref, g_ref, ca_ref, wa2_ref, ba2_ref, nw_ref, y_ref, s_ref):
    tb = qk_ref.shape[0]
    hk = GLA_HEADS * GLA_DK
    hv = GLA_HEADS * GLA_DV
    L = CHUNK

    @pl.when(pl.program_id(0) == 0)
    def _():
        s_ref[...] = jnp.zeros_like(s_ref)

    tri = (lax.broadcasted_iota(jnp.int32, (L, L), 0) >= lax.broadcasted_iota(jnp.int32, (L, L), 1))
    tri_bf = tri.astype(BF16)
    tri4 = jnp.concatenate([tri] * GLA_HEADS, axis=0)
    lane_head = lax.broadcasted_iota(jnp.int32, (L, hk), 1) // GLA_DK
    bd_mask = (lax.broadcasted_iota(jnp.int32, (hk, hv), 0) // GLA_DK
               == lax.broadcasted_iota(jnp.int32, (hk, hv), 1) // GLA_DV)
    ones_bf = jnp.ones((L, LANES), BF16)
    scale = GLA_DK ** -0.5

    for c in range(tb // L):
        rs = slice(c * L, (c + 1) * L)
        q = qk_ref[rs, 0:hk].astype(F32) * scale
        k = qk_ref[rs, hk:2 * hk].astype(F32)
        v = v_ref[rs, :]
        x = jnp.dot(ca_ref[rs, :], wa2_ref[...], preferred_element_type=F32) + ba2_ref[...]
        la = (jnp.minimum(x, 0.0) - jnp.log(1.0 + jnp.exp(-jnp.abs(x)))) * (1.0 / GLA_TAU)
        la_hi, la_lo = _split_bf16(la)
        bcum = (jnp.dot(tri_bf, la_hi, preferred_element_type=F32)
                + jnp.dot(tri_bf, la_lo, preferred_element_type=F32))
        blast = bcum[L - 1:L, :]
        q_dec = q * jnp.exp(bcum)
        k_inv = (k * jnp.exp(-bcum)).astype(BF16)
        k_end = (k * jnp.exp(blast - bcum)).astype(BF16)

        q4 = jnp.concatenate(
            [jnp.where(lane_head == hd, q_dec, 0.0) for hd in range(GLA_HEADS)], axis=0).astype(BF16)
        sc = lax.dot_general(q4, k_inv, _NT, preferred_element_type=F32)
        sc = jnp.where(tri4, sc, 0.0).astype(BF16)
        oi = jnp.dot(sc, v, preferred_element_type=F32)
        o_intra = jnp.concatenate(
            [oi[hd * L:(hd + 1) * L, hd * GLA_DV:(hd + 1) * GLA_DV] for hd in range(GLA_HEADS)], axis=1)

        s = s_ref[...]
        o = o_intra + jnp.dot(q_dec.astype(BF16), s.astype(BF16), preferred_element_type=F32)

        kv = lax.dot_general(k_end, v, _TN, preferred_element_type=F32)
        bl_col = (lax.dot_general(la_hi, ones_bf, _TN, preferred_element_type=F32)
                  + lax.dot_general(la_lo, ones_bf, _TN, preferred_element_type=F32))
        dcol = jnp.exp(bl_col)
        s_ref[...] = s * jnp.concatenate([dcol] * (hv // LANES), axis=1) + jnp.where(bd_mask, kv, 0.0)

        parts = []
        for hd in range(GLA_HEADS):
            oh = o[:, hd * GLA_DV:(hd + 1) * GLA_DV]
            parts.append(oh * lax.rsqrt(jnp.mean(oh * oh, axis=-1, keepdims=True) + RMS_EPS))
        on = jnp.concatenate(parts, axis=1) * nw_ref[...]
        y_ref[rs, :] = (on * _silu(g_ref[rs, :].astype(F32))).astype(y_ref.dtype)


def _mixer_gla(z, wa2_pad_bf, ba2, nw, tb=256):
    t = z.shape[0]
    tb = min(tb, t)
    w = GROUP_W
    hk = GLA_HEADS * GLA_DK
    return pl.pallas_call(
        _gla_kernel,
        grid=(t // tb,),
        in_specs=[pl.BlockSpec((tb, w), lambda i: (i, ZB_CQK)),
                  pl.BlockSpec((tb, w), lambda i: (i, ZB_CV)),
                  pl.BlockSpec((tb, w), lambda i: (i, ZB_CG)),
                  pl.BlockSpec((tb, LANES), lambda i: (i, ZB_CA128)),
                  pl.BlockSpec((LANES, hk), lambda i: (0, 0)),
                  pl.BlockSpec((1, hk), lambda i: (0, 0)),
                  pl.BlockSpec((1, w), lambda i: (0, 0))],
        out_specs=pl.BlockSpec((tb, w), lambda i: (i, 0)),
        out_shape=jax.ShapeDtypeStruct((t, w), BF16),
        scratch_shapes=[pltpu.VMEM((hk, w), F32)],
        compiler_params=_cparams(("arbitrary",)),
        name="mixer_gla",
    )(z, z, z, z, wa2_pad_bf, ba2.reshape(1, hk), nw.reshape(1, w))


ATT_TQ = 512
ATT_SUB = 128
ATT_KW = ATT_SUB + ATT_PREV_CHUNKS * CHUNK


def _att_kernel(q_ref, kp_ref, kc_ref, vp_ref, vc_ref, bias_ref, y_ref, kcat, vcat):
    i = pl.program_id(0)
    kcat[0:ATT_TQ, :] = kp_ref[...]
    kcat[ATT_TQ:2 * ATT_TQ, :] = kc_ref[...]
    vcat[0:ATT_TQ, :] = vp_ref[...]
    vcat[ATT_TQ:2 * ATT_TQ, :] = vc_ref[...]
    col = lax.broadcasted_iota(jnp.int32, (ATT_SUB, ATT_KW), 1)
    scale = ATT_HD ** -0.5
    for j in range(ATT_TQ // ATT_SUB):
        r0 = j * ATT_SUB
        n_invalid = jnp.where(i == 0, ATT_TQ - r0, 0)
        dead = col < n_invalid
        outs = []
        for hd in range(ATT_HEADS):
            ls = slice(hd * ATT_HD, (hd + 1) * ATT_HD)
            qh = q_ref[r0:r0 + ATT_SUB, ls]
            kh = kcat[r0:r0 + ATT_KW, ls]
            vh = vcat[r0:r0 + ATT_KW, ls]
            s = lax.dot_general(qh, kh, _NT, preferred_element_type=F32) * scale + bias_ref[hd]
            s = jnp.where(dead, -jnp.inf, s)
            m = jnp.max(s, axis=-1, keepdims=True)
            p = jnp.exp(s - m)
            l = jnp.sum(p, axis=-1, keepdims=True)
            o = jnp.dot(p.astype(BF16), vh, preferred_element_type=F32)
            outs.append(o / l)
        y_ref[r0:r0 + ATT_SUB, :] = jnp.concatenate(outs, axis=1).astype(y_ref.dtype)


def _att_bias_table(rel_bias):
    nh = rel_bias.shape[0]
    span = ATT_PREV_CHUNKS * CHUNK
    period = ATT_KW + ATT_SUB
    edge = rel_bias[:, 2 * REL_CLIP:2 * REL_CLIP + 1]
    n_flat = span - REL_CLIP + 1
    ramp = rel_bias[:, 2 * REL_CLIP - (ATT_KW - n_flat):2 * REL_CLIP][:, ::-1]
    vec = jnp.concatenate([jnp.broadcast_to(edge, (nh, n_flat)), ramp,
                           jnp.broadcast_to(edge, (nh, period - ATT_KW))], axis=1)
    tab = jnp.tile(vec, (1, ATT_SUB))[:, :ATT_SUB * (period - 1)].reshape(nh, ATT_SUB, period - 1)[:, :, :ATT_KW]
    r = np.arange(ATT_SUB)[:, None]
    c = np.arange(ATT_KW)[None, :]
    rel = c - CHUNK * (r // CHUNK)
    valid = (rel >= 0) & (rel < (ATT_PREV_CHUNKS + 1) * CHUNK)
    return jnp.where(jnp.asarray(valid)[None], tab.astype(F32), -jnp.inf)


def _mixer_att(z, bias_tab):
    t = z.shape[0]
    w = GROUP_W
    tq = ATT_TQ

    def prev(i):
        return jnp.maximum(i - 1, 0)

    return pl.pallas_call(
        _att_kernel,
        grid=(t // tq,),
        in_specs=[pl.BlockSpec((tq, w), lambda i: (i, ZB_DQ)),
                  pl.BlockSpec((tq, w), lambda i: (prev(i), ZB_DK)),
                  pl.BlockSpec((tq, w), lambda i: (i, ZB_DK)),
                  pl.BlockSpec((tq, w), lambda i: (prev(i), ZB_DV)),
                  pl.BlockSpec((tq, w), lambda i: (i, ZB_DV)),
                  pl.BlockSpec((ATT_HEADS, ATT_SUB, ATT_KW), lambda i: (0, 0, 0))],
        out_specs=pl.BlockSpec((tq, w), lambda i: (i, 0)),
        out_shape=jax.ShapeDtypeStruct((t, w), BF16),
        scratch_shapes=[pltpu.VMEM((2 * tq, w), BF16), pltpu.VMEM((2 * tq, w), BF16)],
        compiler_params=_cparams(("arbitrary",)),
        name="mixer_att",
    )(z, z, z, z, z, bias_tab)


R_ROWS = 32
PAIRS = ((0, 1), (0, 2), (0, 3), (1, 3), (1, 2), (2, 3))
SLOT0 = (0, 0, 0, 1, 1, 3)
SLOT1 = (1, 2, 3, 3, 2, 2)
N_PAIRS = len(PAIRS)
N_CLASSES = N_GROUPS * N_PAIRS
C_ROWS = 32


def _outproj_kernel(ya_ref, yb_ref, yc_ref, yd_ref, w_ref, h_ref, g_ref, b_ref, wr_ref, br_ref,
                    h1_ref, meta_ref, counts_ref, cnt_ref):
    w = GROUP_W
    acc = jnp.dot(ya_ref[...], w_ref[0:w, :], preferred_element_type=F32)
    acc = acc + jnp.dot(yb_ref[...], w_ref[w:2 * w, :], preferred_element_type=F32)
    acc = acc + jnp.dot(yc_ref[...], w_ref[2 * w:3 * w, :], preferred_element_type=F32)
    acc = acc + jnp.dot(yd_ref[...], w_ref[3 * w:4 * w, :], preferred_element_type=F32)
    tm = ya_ref.shape[0]
    h1 = _layer_norm(DN_ALPHA * _rows_load(h_ref, tm) + acc, g_ref[...], b_ref[...])
    _rows_store(h1_ref, h1)

    h1_hi, h1_lo = _split_bf16(h1)
    lt = (lax.dot_general(wr_ref[0], h1_hi, _NT, preferred_element_type=F32)
          + lax.dot_general(wr_ref[0], h1_lo, _NT, preferred_element_type=F32)
          + lax.dot_general(wr_ref[1], h1_hi, _NT, preferred_element_type=F32)) + br_ref[:, 0:1]
    ng = N_GROUPS
    epg = EXPERTS_PER_GROUP
    lg = lt[0:ng, :]
    gm = jnp.max(lg, axis=0, keepdims=True)
    g_val = 1.0 / jnp.sum(jnp.exp(lg - gm), axis=0, keepdims=True)
    gi = jnp.full(gm.shape, ng - 1, jnp.int32)
    for g in range(ng - 2, -1, -1):
        gi = jnp.where(lg[g:g + 1, :] == gm, g, gi)
    e = []
    for j in range(epg):
        ej = lt[ng + (ng - 1) * epg + j:ng + (ng - 1) * epg + j + 1, :]
        for g in range(ng - 2, -1, -1):
            ej = jnp.where(gi == g, lt[ng + g * epg + j:ng + g * epg + j + 1, :], ej)
        e.append(ej)
    v1 = jnp.maximum(jnp.maximum(e[0], e[1]), jnp.maximum(e[2], e[3]))
    i1 = jnp.full(v1.shape, epg - 1, jnp.int32)
    for j in range(epg - 2, -1, -1):
        i1 = jnp.where(e[j] == v1, j, i1)
    neg = -jnp.inf
    e2 = [jnp.where(i1 == j, neg, e[j]) for j in range(epg)]
    v2 = jnp.maximum(jnp.maximum(e2[0], e2[1]), jnp.maximum(e2[2], e2[3]))
    i2 = jnp.full(v1.shape, epg - 1, jnp.int32)
    for j in range(epg - 2, -1, -1):
        i2 = jnp.where(e2[j] == v2, j, i2)
    t2 = jnp.exp(v2 - v1)
    w1 = g_val / (1.0 + t2)
    w2 = g_val * t2 / (1.0 + t2)
    lo_e = jnp.minimum(i1, i2)
    hi_e = jnp.maximum(i1, i2)
    pkey = lo_e * epg + hi_e
    pair = jnp.zeros_like(pkey)
    for pi, (pa, pb) in enumerate(PAIRS):
        pair = jnp.where(pkey == pa * epg + pb, pi, pair)
    cls = gi * N_PAIRS + pair
    wloc = [jnp.where(i1 == j, w1, 0.0) + jnp.where(i2 == j, w2, 0.0) for j in range(epg)]
    ws0 = jnp.zeros_like(w1)
    ws1 = jnp.zeros_like(w1)
    for pi in range(N_PAIRS):
        ws0 = jnp.where(pair == pi, wloc[SLOT0[pi]], ws0)
        ws1 = jnp.where(pair == pi, wloc[SLOT1[pi]], ws1)

    @pl.when(pl.program_id(0) == 0)
    def _():
        cnt_ref[...] = jnp.zeros_like(cnt_ref)

    tm = lt.shape[1]
    crow = lax.broadcasted_iota(jnp.int32, (C_ROWS, tm), 0)
    ohf = jnp.where(crow == cls, 1.0, 0.0)
    upper = (lax.broadcasted_iota(jnp.int32, (tm, tm), 0)
             < lax.broadcasted_iota(jnp.int32, (tm, tm), 1)).astype(BF16)
    before = jnp.dot(ohf.astype(BF16), upper, preferred_element_type=F32)
    carry = cnt_ref[...]
    rank = jnp.sum(ohf * (before + carry[:, 0:1]), axis=0, keepdims=True)
    cnt_new = carry + jnp.sum(ohf, axis=1, keepdims=True)
    cnt_ref[...] = cnt_new
    counts_ref[...] = cnt_new
    meta_ref[...] = jnp.concatenate(
        [cls.astype(F32), ws0, ws1, rank, jnp.zeros((SUBLANES - 4, tm), F32)], axis=0)


def _outproj_router(ya, yb, yc, yd, w_out_bf, h, t, g, b, wr_t_bf, br_col, tm=512):
    d = D_MODEL
    w = GROUP_W
    tm = min(tm, t)

    def ys():
        return pl.BlockSpec((tm, w), lambda i: (i, 0))

    def rows():
        return pl.BlockSpec((tm * ROW_CHUNKS, LANES), lambda i: (i, 0))

    return pl.pallas_call(
        _outproj_kernel,
        grid=(t // tm,),
        in_specs=[ys(), ys(), ys(), ys(),
                  pl.BlockSpec((d, d), lambda i: (0, 0)),
                  rows(),
                  pl.BlockSpec((1, d), lambda i: (0, 0)),
                  pl.BlockSpec((1, d), lambda i: (0, 0)),
                  pl.BlockSpec((2, R_ROWS, d), lambda i: (0, 0, 0)),
                  pl.BlockSpec((R_ROWS, LANES), lambda i: (0, 0))],
        out_specs=[rows(),
                   pl.BlockSpec((SUBLANES, tm), lambda i: (0, i)),
                   pl.BlockSpec((C_ROWS, LANES), lambda i: (0, 0))],
        out_shape=[jax.ShapeDtypeStruct((t * ROW_CHUNKS, LANES), F32),
                   jax.ShapeDtypeStruct((SUBLANES, t), F32),
                   jax.ShapeDtypeStruct((C_ROWS, LANES), F32)],
        scratch_shapes=[pltpu.VMEM((C_ROWS, LANES), F32)],
        compiler_params=_cparams(("arbitrary",)),
        name="outproj_router",
    )(ya, yb, yc, yd, w_out_bf, h, g.reshape(1, d), b.reshape(1, d), wr_t_bf, br_col)


MOE_TM = 256
DUMP_TOKENS = 2 * MOE_TM


def _moe_kernel(te0_ref, te1_ref, nt_ref,
                gtok_ref, gtok_next_ref, stok_ref, w_ref, wg0_ref, wu0_ref, wd0_ref, wg1_ref, wu1_ref, wd1_ref,
                g_ref, b_ref, hin_ref, hout_ref, xg, yo, gsem, ssem, *, n_tok):
    k = pl.program_id(0)
    nt = nt_ref[0]
    tm = MOE_TM
    rc = ROW_CHUNKS
    slot = lax.rem(k, 2)
    other = 1 - slot

    def gather_row(idx_ref, s, r):
        src = hin_ref.at[pl.ds(pl.multiple_of(idx_ref[0, 0, r] * rc, rc), rc)]
        return pltpu.make_async_copy(src, xg.at[s, pl.ds(r * rc, rc)], gsem.at[s])

    def scatter_row(s, r):
        dst = hout_ref.at[pl.ds(pl.multiple_of(stok_ref[0, 0, r] * rc, rc), rc)]
        return pltpu.make_async_copy(yo.at[s, pl.ds(r * rc, rc)], dst, ssem.at[s])

    def wait_gather(s):
        pltpu.make_async_copy(hin_ref.at[pl.ds(0, tm * rc)], xg.at[s], gsem.at[s]).wait()

    def wait_scatter(s):
        pltpu.make_async_copy(yo.at[s], hout_ref.at[pl.ds(0, tm * rc)], ssem.at[s]).wait()

    @pl.when(k == 0)
    def _():
        yo[...] = jnp.zeros_like(yo)
        for s in range(2):
            pltpu.make_async_copy(yo.at[s], hout_ref.at[pl.ds((n_tok + s * tm) * rc, tm * rc)], ssem.at[s]).start()
        for s in range(2):
            wait_scatter(s)
        for r in range(tm):
            gather_row(gtok_ref, 0, r).start()

    @pl.when(k < nt)
    def _():
        wait_gather(slot)

        @pl.when(k >= 2)
        def _():
            wait_scatter(slot)

        for r in range(tm):
            gather_row(gtok_next_ref, other, r).start()

        x = _rows_load(xg.at[slot], tm)
        xb = x.astype(BF16)
        h0 = _silu(jnp.dot(xb, wg0_ref[0], preferred_element_type=F32)) * jnp.dot(
            xb, wu0_ref[0], preferred_element_type=F32)
        h1 = _silu(jnp.dot(xb, wg1_ref[0], preferred_element_type=F32)) * jnp.dot(
            xb, wu1_ref[0], preferred_element_type=F32)
        ffn = jnp.dot((h0 * w_ref[:, 1:2]).astype(BF16), wd0_ref[0], preferred_element_type=F32)
        ffn = ffn + jnp.dot((h1 * w_ref[:, 2:3]).astype(BF16), wd1_ref[0], preferred_element_type=F32)
        _rows_store(yo.at[slot], _layer_norm(DN_ALPHA * x + ffn, g_ref[...], b_ref[...]))
        for r in range(tm):
            scatter_row(slot, r).start()

        @pl.when(k == nt - 1)
        def _():
            wait_gather(other)

            @pl.when(k >= 1)
            def _():
                wait_scatter(other)
            wait_scatter(slot)


def _moe_sorted(h1, n_tok, gtok, stok, wsort, te0, te1, nt, wg_bf, wu_bf, wd_bf, g, b):
    d = D_MODEL
    tm = MOE_TM
    de = wg_bf.shape[2]
    nt_max = gtok.shape[0] // tm

    def wspec(shape, slot):
        if slot == 0:
            return pl.BlockSpec(shape, lambda k, te0, te1, nt: (te0[k], 0, 0))
        return pl.BlockSpec(shape, lambda k, te0, te1, nt: (te1[k], 0, 0))

    def cur(k, te0, te1, nt):
        return jnp.minimum(k, nt[0] - 1)

    def nxt(k, te0, te1, nt):
        return jnp.minimum(k + 1, nt[0] - 1)

    def idx_spec(which):
        return pl.BlockSpec((1, 1, tm), lambda *a: (which(*a), 0, 0), memory_space=pltpu.SMEM)

    return pl.pallas_call(
        functools.partial(_moe_kernel, n_tok=n_tok),
        grid_spec=pltpu.PrefetchScalarGridSpec(
            num_scalar_prefetch=3,
            grid=(nt_max,),
            in_specs=[idx_spec(cur), idx_spec(nxt), idx_spec(cur),
                      pl.BlockSpec((tm, wsort.shape[1]), lambda *a: (cur(*a), 0)),
                      wspec((1, d, de), 0), wspec((1, d, de), 0), wspec((1, de, d), 0),
                      wspec((1, d, de), 1), wspec((1, d, de), 1), wspec((1, de, d), 1),
                      pl.BlockSpec((1, d), lambda k, *_: (0, 0)),
                      pl.BlockSpec((1, d), lambda k, *_: (0, 0)),
                      pl.BlockSpec(memory_space=pl.ANY)],
            out_specs=pl.BlockSpec(memory_space=pl.ANY),
            scratch_shapes=[pltpu.VMEM((2, tm * ROW_CHUNKS, LANES), F32),
                            pltpu.VMEM((2, tm * ROW_CHUNKS, LANES), F32),
                            pltpu.SemaphoreType.DMA((2,)),
                            pltpu.SemaphoreType.DMA((2,))]),
        out_shape=jax.ShapeDtypeStruct(((n_tok + DUMP_TOKENS) * ROW_CHUNKS, LANES), F32),
        compiler_params=_cparams(("arbitrary",)),
        name="moe_sorted",
    )(te0, te1, nt, gtok.reshape(nt_max, 1, tm), gtok.reshape(nt_max, 1, tm), stok.reshape(nt_max, 1, tm), wsort,
      wg_bf, wu_bf, wd_bf, wg_bf, wu_bf, wd_bf, g.reshape(1, d), b.reshape(1, d), h1)


def _moe_plan(meta, counts, t):
    tm = MOE_TM
    nt_max = t // tm + N_CLASSES
    n_pad = nt_max * tm
    cls = meta[0].astype(jnp.int32)
    rank = meta[3].astype(jnp.int32)
    cnt = counts[:N_CLASSES, 0].astype(jnp.int32)
    tiles = (cnt + tm - 1) // tm
    tile_end = jnp.cumsum(tiles)
    tile_start = tile_end - tiles
    n_tiles = tile_end[-1]
    onehot = (cls[:, None] == jnp.arange(N_CLASSES)[None, :]).astype(F32)
    pos = jnp.dot(onehot, tile_start.astype(F32)).astype(jnp.int32) * tm + rank
    k = jnp.arange(nt_max)
    ksrc = jnp.minimum(k, jnp.maximum(n_tiles - 1, 0))
    tcls_oh = ((jnp.sum(ksrc[:, None] >= tile_end[None, :], axis=1))[:, None]
               == jnp.arange(N_CLASSES)[None, :]).astype(jnp.int32)
    tcls = jnp.sum(tcls_oh * jnp.arange(N_CLASSES)[None, :], axis=1)
    grp = tcls // N_PAIRS
    pr_oh = ((tcls % N_PAIRS)[:, None] == jnp.arange(N_PAIRS)[None, :]).astype(jnp.int32)
    te0 = grp * EXPERTS_PER_GROUP + jnp.sum(pr_oh * jnp.asarray(SLOT0, jnp.int32)[None, :], axis=1)
    te1 = grp * EXPERTS_PER_GROUP + jnp.sum(pr_oh * jnp.asarray(SLOT1, jnp.int32)[None, :], axis=1)
    payload = jnp.stack([jnp.arange(t, dtype=F32), meta[1], meta[2], jnp.ones((t,), F32)], axis=1)
    wsort = jnp.zeros((n_pad, 4), F32).at[pos].set(payload)
    gtok = wsort[:, 0].astype(jnp.int32)
    row = jnp.arange(n_pad, dtype=jnp.int32)
    dump = t + lax.rem(row // tm, 2) * tm + lax.rem(row, tm)
    stok = jnp.where(wsort[:, 3] > 0.5, gtok, dump)
    return gtok, stok, wsort, te0.astype(jnp.int32), te1.astype(jnp.int32), n_tiles.astype(jnp.int32).reshape(1)


W_IN_CA0 = 4096
W_IN_COLS = 5648


def _prep_w_in_kernel(w_ref, o_ref):
    ca1 = W_IN_CA0 + GLA_RANK
    n_tail = W_IN_COLS - ca1
    rb = w_ref.shape[1]
    o_ref[0, :, 0:W_IN_CA0] = w_ref[0, :, 0:W_IN_CA0].astype(BF16)
    rest = w_ref[0, :, W_IN_CA0:W_IN_COLS]
    o_ref[0, :, W_IN_CA0:W_IN_CA0 + n_tail] = rest[:, GLA_RANK:].astype(BF16)
    pad = Z_W - W_IN_CA0 - n_tail - GLA_RANK
    o_ref[0, :, W_IN_CA0 + n_tail:Z_W] = jnp.concatenate(
        [rest[:, 0:GLA_RANK], jnp.zeros((rb, pad), F32)], axis=1).astype(BF16)


def _prep_w_in(w_in, rb=256):
    nl, d, n = w_in.shape
    return pl.pallas_call(
        _prep_w_in_kernel,
        grid=(nl, d // rb),
        in_specs=[pl.BlockSpec((1, rb, n), lambda l, i: (l, i, 0))],
        out_specs=pl.BlockSpec((1, rb, Z_W), lambda l, i: (l, i, 0)),
        out_shape=jax.ShapeDtypeStruct((nl, d, Z_W), BF16),
        compiler_params=_cparams(("arbitrary", "arbitrary")),
        name="prep_w_in",
    )(w_in)


def _prep_router(w_rg_l, b_rg_l, w_re_l, b_re_l):
    d = w_rg_l.shape[0]
    wr = jnp.concatenate([w_rg_l, w_re_l], axis=1)
    wr_t = jnp.zeros((R_ROWS, d), F32).at[:wr.shape[1]].set(wr.T)
    wr_hi = wr_t.astype(BF16)
    wr_lo = (wr_t - wr_hi.astype(F32)).astype(BF16)
    wr_t = jnp.stack([wr_hi, wr_lo], axis=0)
    br = jnp.concatenate([b_rg_l, b_re_l])
    br_col = jnp.zeros((R_ROWS, LANES), F32).at[:br.shape[0], :].set(br[:, None])
    return wr_t, br_col


def _layer(l, h, t, p):
    z = _inproj(h, t, p["w_in_bf"][l])
    ya, yb = _mixer_ab(z, p["conv_a_w"][l], p["conv_a_b"][l], p["lru_wa"][l].astype(BF16), p["lru_ba"][l],
                       p["lru_wx"][l].astype(BF16), p["lru_bx"][l], p["lru_lam"][l], p["conv_b_w"][l])
    hk = GLA_HEADS * GLA_DK
    wa2_pad = jnp.zeros((LANES, hk), F32).at[:GLA_RANK].set(p["gla_wa2"][l]).astype(BF16)
    yc = _mixer_gla(z, wa2_pad, p["gla_ba2"][l], p["gla_norm_w"][l])
    yd = _mixer_att(z, _att_bias_table(p["att_rel_bias"][l]))
    wr_t, br_col = _prep_router(p["w_rg"][l], p["b_rg"][l], p["w_re"][l], p["b_re"][l])
    h1, meta, counts = _outproj_router(ya, yb, yc, yd, p["w_out"][l].astype(BF16), h, t, p["ln1_g"][l],
                                       p["ln1_b"][l], wr_t, br_col)
    gtok, stok, wsort, te0, te1, nt = _moe_plan(meta, counts, t)
    return _moe_sorted(h1, t, gtok, stok, wsort, te0, te1, nt, p["w_gate"][l].astype(BF16),
                       p["w_up"][l].astype(BF16), p["w_down"][l].astype(BF16), p["ln2_g"][l], p["ln2_b"][l])


def kernel(x, ln0_g, ln0_b, w_in, conv_a_w, conv_a_b, lru_wa, lru_ba, lru_wx, lru_bx, lru_lam, conv_b_w,
           gla_wa2, gla_ba2, gla_norm_w, att_rel_bias, w_out, ln1_g, ln1_b, w_rg, b_rg, w_re, b_re,
           w_gate, w_up, w_down, ln2_g, ln2_b):
    bsz, seq, d = x.shape
    p = dict(w_in=w_in, conv_a_w=conv_a_w, conv_a_b=conv_a_b, lru_wa=lru_wa, lru_ba=lru_ba, lru_wx=lru_wx,
             lru_bx=lru_bx, lru_lam=lru_lam, conv_b_w=conv_b_w, gla_wa2=gla_wa2, gla_ba2=gla_ba2,
             gla_norm_w=gla_norm_w, att_rel_bias=att_rel_bias, w_out=w_out, ln1_g=ln1_g, ln1_b=ln1_b,
             w_rg=w_rg, b_rg=b_rg, w_re=w_re, b_re=b_re, w_gate=w_gate, w_up=w_up, w_down=w_down,
             ln2_g=ln2_g, ln2_b=ln2_b)
    p["w_in_bf"] = _prep_w_in(w_in)
    outs = []
    for bi in range(bsz):
        h = _ln_rows(x[bi], ln0_g, ln0_b)
        for l in range(DEPTH):
            h = _layer(l, h, seq, p)
        outs.append(_stream_to_2d(h, seq))
    return jnp.stack(outs, axis=0)
```
